```python
import jax, jax.numpy as jnp
from jax import lax
import numpy as np

D_MODEL = 4096
BATCH = 4
SEQ = 2048
DEPTH = 2
DEC_BATCH = 128
DEC_SEQ = 8
PAST_LEN = 16384
PAGE_SIZE = 128

N_AB_LAYERS = (DEPTH + 1) // 2
N_C_LAYERS = DEPTH // 2
RET_WIDTH = D_MODEL // 2
ML_WIDTH = D_MODEL - RET_WIDTH
RET_HEADS = 8
RET_DK = RET_WIDTH // RET_HEADS
RET_DV = RET_WIDTH // RET_HEADS
ML_HEADS = 8
ML_DK = ML_WIDTH // ML_HEADS
ML_DV = ML_WIDTH // ML_HEADS
HG_DK = 128
HG_HEADS = D_MODEL // HG_DK
HG_DV = D_MODEL // HG_HEADS
HG_WIDTH = HG_HEADS * HG_DK
D_FF = ((8 * D_MODEL + 3 * 256 - 1) // (3 * 256)) * 256
RET_CHUNK = 128
ML_CHUNK = 128
HG_CHUNK = 32
ROPE_BASE = 10000.0
EPS = 1e-6
AB_COLS = 4 * RET_WIDTH + 4 * ML_WIDTH + 2 * ML_HEADS
C_COLS = 2 * HG_WIDTH + 2 * HG_HEADS * HG_DV

kernel_name = 'hybrid_retnet_mlstm_hgrn2_step'


def _rmsnorm(x, w):
    xf = x.astype(jnp.float32)
    y = xf * lax.rsqrt(jnp.mean(xf * xf, axis=-1, keepdims=True) + EPS)
    return (y * w.astype(jnp.float32)).astype(x.dtype)


def _head_norm(x):
    return x * lax.rsqrt(jnp.mean(x * x, axis=-1, keepdims=True) + EPS)


def _to_heads(x, n_heads):
    b, t, _ = x.shape
    return x.reshape(b, t, n_heads, -1).transpose(0, 2, 1, 3).astype(jnp.float32)


def _from_heads(x):
    b, h, t, d = x.shape
    return x.transpose(0, 2, 1, 3).reshape(b, t, h * d)


def _chunk_len(t, c):
    return c if t % c == 0 else t


def _chunks(x, l):
    b, h, t = x.shape[:3]
    return jnp.moveaxis(x.reshape((b, h, t // l, l) + x.shape[3:]), 2, 0)


def _unchunk(x):
    n, b, h, l = x.shape[:4]
    return jnp.moveaxis(x, 0, 2).reshape((b, h, n * l) + x.shape[4:])


def _rotary(x, pos):
    half = x.shape[-1] // 2
    inv_freq = ROPE_BASE ** (-jnp.arange(half, dtype=jnp.float32) / half)
    ang = pos[:, None] * inv_freq[None, :]
    cos, sin = jnp.cos(ang), jnp.sin(ang)
    x1, x2 = x[..., :half], x[..., half:]
    return jnp.concatenate([x1 * cos - x2 * sin, x1 * sin + x2 * cos], axis=-1)


def _retention(q, k, v, s0):
    t = q.shape[2]
    l = _chunk_len(t, RET_CHUNK)
    log_gamma = jnp.log1p(-jnp.exp2(-5.0 - jnp.arange(RET_HEADS, dtype=jnp.float32)))[:, None]
    idx = jnp.arange(l, dtype=jnp.float32)
    diff = idx[:, None] - idx[None, :]
    decay = jnp.exp(jnp.where(diff >= 0, log_gamma[:, :, None] * diff, -jnp.inf))
    q_decay = jnp.exp(log_gamma * (idx + 1.0))
    k_decay = jnp.exp(log_gamma * (l - 1.0 - idx))
    chunk_decay = jnp.exp(log_gamma[:, 0] * l)

    def step(s, inp):
        qc, kc, vc = inp
        att = jnp.einsum('bhtd,bhsd->bhts', qc, kc) * decay
        o = (jnp.einsum('bhts,bhse->bhte', att, vc)
             + jnp.einsum('bhtd,bhde->bhte', qc, s) * q_decay[:, :, None])
        s_new = (chunk_decay[:, None, None] * s
                 + jnp.einsum('bhsd,bhse->bhde', kc * k_decay[:, :, None], vc))
        return s_new, o

    s_end, o = lax.scan(step, s0, (_chunks(q, l), _chunks(k, l), _chunks(v, l)))
    return _unchunk(o), s_end


def _mlstm(q, k, v, ig, lf, c0, n0, m0):
    t = q.shape[2]
    l = _chunk_len(t, ML_CHUNK)
    causal = jnp.tril(jnp.ones((l, l), dtype=bool))

    def step(carry, inp):
        c, n, m = carry
        qc, kc, vc, igc, lfc = inp
        f_cum = jnp.cumsum(lfc, axis=-1)
        a = igc - f_cum
        m_t = f_cum + jnp.maximum(m[..., None], lax.cummax(a, axis=2))
        log_d = jnp.where(causal, (f_cum - m_t)[..., :, None] + a[..., None, :], -jnp.inf)
        s = jnp.einsum('bhtd,bhsd->bhts', qc, kc) * jnp.exp(log_d)
        inter = jnp.exp(f_cum + m[..., None] - m_t)
        num = (inter[..., None] * jnp.einsum('bhtd,bhde->bhte', qc, c)
               + jnp.einsum('bhts,bhse->bhte', s, vc))
        den = inter * jnp.einsum('bhtd,bhd->bht', qc, n) + jnp.sum(s, axis=-1)
        h = num / jnp.maximum(jnp.abs(den), jnp.exp(-m_t))[..., None]
        m_end = m_t[..., -1]
        w = jnp.exp(f_cum[..., -1:] - m_end[..., None] + a)
        carry_decay = jnp.exp(f_cum[..., -1] + m - m_end)
        c_new = carry_decay[..., None, None] * c + jnp.einsum('bhs,bhsd,bhse->bhde', w, kc, vc)
        n_new = carry_decay[..., None] * n + jnp.einsum('bhs,bhsd->bhd', w, kc)
        return (c_new, n_new, m_end), h

    (c_end, n_end, m_end), h = lax.scan(
        step, (c0, n0, m0),
        (_chunks(q, l), _chunks(k, l), _chunks(v, l), _chunks(ig, l), _chunks(lf, l)))
    return _unchunk(h), c_end, n_end, m_end


def _hgrn2(q, k, logf, v, s0):
    t = q.shape[2]
    l = _chunk_len(t, HG_CHUNK)
    causal = jnp.tril(jnp.ones((l, l), dtype=bool))[:, :, None]

    def step(s, inp):
        qc, kc, gc, vc = inp
        g_cum = jnp.cumsum(gc, axis=2)
        log_w = jnp.where(causal, g_cum[:, :, :, None, :] - g_cum[:, :, None, :, :], -jnp.inf)
        att = jnp.einsum('bhtd,bhsd,bhtsd->bhts', qc, kc, jnp.exp(log_w))
        o = (jnp.einsum('bhtd,bhde->bhte', qc * jnp.exp(g_cum), s)
             + jnp.einsum('bhts,bhse->bhte', att, vc))
        g_end = g_cum[:, :, -1]
        s_new = (jnp.exp(g_end)[..., None] * s
                 + jnp.einsum('bhsd,bhse->bhde', kc * jnp.exp(g_end[:, :, None] - g_cum), vc))
        return s_new, o

    s_end, o = lax.scan(step, s0, (_chunks(q, l), _chunks(k, l), _chunks(logf, l), _chunks(v, l)))
    return _unchunk(o), s_end


def _ab_mixer(h, pos, w_in, b_if, ml_norm_w, w_out, s_ret, c_ml, n_ml, m_ml):
    f32 = jnp.float32
    proj = jnp.einsum('btd,dc->btc', h, w_in)
    rw, mw = RET_WIDTH, ML_WIDTH
    offs = [int(o) for o in np.cumsum([rw, rw, rw, rw, mw, mw, mw, mw])]
    rq, rk, rv, rg, mq, mk, mv, mo, gates = jnp.split(proj, offs, axis=-1)
    rq = _rotary(_to_heads(rq, RET_HEADS), pos)
    rk = _rotary(_to_heads(rk, RET_HEADS), pos) * RET_DK ** -0.5
    r_o, s_ret_new = _retention(rq, rk, _to_heads(rv, RET_HEADS), s_ret.astype(f32))
    r_out = jax.nn.silu(rg.astype(f32)) * _from_heads(_head_norm(r_o))
    gates = gates.astype(f32) + b_if.astype(f32)
    ig = jnp.swapaxes(gates[..., :ML_HEADS], 1, 2)
    lf = jax.nn.log_sigmoid(jnp.swapaxes(gates[..., ML_HEADS:], 1, 2))
    m_h, c_new, n_new, m_new = _mlstm(
        _to_heads(mq, ML_HEADS), _to_heads(mk, ML_HEADS) * ML_DK ** -0.5, _to_heads(mv, ML_HEADS),
        ig, lf, c_ml.astype(f32), n_ml.astype(f32), m_ml.astype(f32))
    m_out = jax.nn.sigmoid(mo.astype(f32)) * (_from_heads(_head_norm(m_h)) * ml_norm_w.astype(f32))
    mixed = jnp.concatenate([r_out, m_out], axis=-1).astype(h.dtype)
    return jnp.einsum('btc,cd->btd', mixed, w_out), (s_ret_new, c_new, n_new, m_new)


def _c_mixer(h, lb, w_in, g_norm_w, w_out, s_hg):
    f32 = jnp.float32
    proj = jnp.einsum('btd,dc->btc', h, w_in)
    q, f, i, g = jnp.split(proj, [HG_WIDTH, 2 * HG_WIDTH, 2 * HG_WIDTH + HG_HEADS * HG_DV], axis=-1)
    fg = _to_heads(lb + (1.0 - lb) * jax.nn.sigmoid(f.astype(f32)), HG_HEADS)
    q = jax.nn.silu(_to_heads(q, HG_HEADS))
    o, s_new = _hgrn2(q, 1.0 - fg, jnp.log(fg), _to_heads(i, HG_HEADS), s_hg.astype(f32))
    o = _from_heads(_head_norm(o) * g_norm_w.astype(f32)) * jax.nn.silu(g.astype(f32))
    return jnp.einsum('btc,cd->btd', o.astype(h.dtype), w_out), s_new


def _swiglu(h, w_gate, w_up, w_down):
    a = jnp.einsum('btd,df->btf', h, w_gate)
    b = jnp.einsum('btd,df->btf', h, w_up)
    return jnp.einsum('btf,fd->btd', jax.nn.silu(a) * b, w_down)


def _trunk(x, pos, ret0, mc0, mn0, mm0, hg0, norm_mix_w, w_in_ab, b_if_ab, ml_norm_w, w_out_ab,
           w_in_c, lb_logits, hg_norm_w, w_out_c, norm_ffn_w, w_gate, w_up, w_down, norm_final_w):
    cum = jnp.cumsum(jax.nn.softmax(lb_logits.astype(jnp.float32), axis=0), axis=0)
    lower_bounds = cum - cum[0]
    ret_new, mc_new, mn_new, mm_new, hg_new = [], [], [], [], []
    for layer in range(DEPTH):
        j = layer // 2
        hn = _rmsnorm(x, norm_mix_w[layer])
        if layer % 2 == 0:
            mix, (sr, sc, sn, sm) = _ab_mixer(hn, pos, w_in_ab[j], b_if_ab[j], ml_norm_w[j], w_out_ab[j],
                                              ret0[j], mc0[j], mn0[j], mm0[j])
            ret_new.append(sr)
            mc_new.append(sc)
            mn_new.append(sn)
            mm_new.append(sm)
        else:
            mix, sh = _c_mixer(hn, lower_bounds[layer], w_in_c[j], hg_norm_w[j], w_out_c[j], hg0[j])
            hg_new.append(sh)
        x = x + mix.astype(x.dtype)
        x = x + _swiglu(_rmsnorm(x, norm_ffn_w[layer]), w_gate[layer], w_up[layer], w_down[layer]).astype(x.dtype)
    dt = x.dtype
    return (_rmsnorm(x, norm_final_w), jnp.stack(ret_new).astype(dt), jnp.stack(mc_new).astype(dt),
            jnp.stack(mn_new).astype(dt), jnp.stack(mm_new).astype(dt), jnp.stack(hg_new).astype(dt))


def setup_inputs(seed: int = 0) -> dict:
    key = jax.random.key(seed)
    ks = jax.random.split(key, 24)
    f32 = jnp.float32

    def nrm(k, shape, scale):
        return jax.random.normal(k, shape, f32) * scale

    d = D_MODEL
    b_if_ab = jnp.concatenate(
        [nrm(ks[9], (N_AB_LAYERS, ML_HEADS), 0.1),
         jnp.linspace(3.0, 6.0, ML_HEADS, dtype=f32)[None, :] + nrm(ks[10], (N_AB_LAYERS, ML_HEADS), 0.1)],
        axis=-1)
    return {
        'x_prompt': nrm(ks[0], (BATCH, SEQ, d), 1.0),
        'x_sample': nrm(ks[1], (DEC_BATCH, DEC_SEQ, d), 1.0),
        'state_ret': nrm(ks[2], (N_AB_LAYERS, DEC_BATCH, RET_HEADS, RET_DK, RET_DV), 0.5),
        'state_mlstm_C': nrm(ks[3], (N_AB_LAYERS, DEC_BATCH, ML_HEADS, ML_DK, ML_DV), 0.5),
        'state_mlstm_n': nrm(ks[4], (N_AB_LAYERS, DEC_BATCH, ML_HEADS, ML_DK), 0.5),
        'state_mlstm_m': nrm(ks[5], (N_AB_LAYERS, DEC_BATCH, ML_HEADS), 1.0),
        'state_hgrn': nrm(ks[6], (N_C_LAYERS, DEC_BATCH, HG_HEADS, HG_DK, HG_DV), 0.5),
        'norm_mix_w': 1.0 + nrm(ks[7], (DEPTH, d), 0.01),
        'w_in_ab': nrm(ks[8], (N_AB_LAYERS, d, AB_COLS), d ** -0.5),
        'b_if_ab': b_if_ab,
        'ml_norm_w': 1.0 + nrm(ks[11], (N_AB_LAYERS, ML_WIDTH), 0.01),
        'w_out_ab': nrm(ks[12], (N_AB_LAYERS, RET_WIDTH + ML_WIDTH, d), (RET_WIDTH + ML_WIDTH) ** -0.5),
        'w_in_c': nrm(ks[13], (N_C_LAYERS, d, C_COLS), d ** -0.5),
        'lb_logits': nrm(ks[14], (DEPTH, HG_WIDTH), 0.1),
        'hg_norm_w': 1.0 + nrm(ks[15], (N_C_LAYERS, HG_DV), 0.01),
        'w_out_c': nrm(ks[16], (N_C_LAYERS, HG_HEADS * HG_DV, d), (HG_HEADS * HG_DV) ** -0.5),
        'norm_ffn_w': 1.0 + nrm(ks[17], (DEPTH, d), 0.01),
        'w_gate': nrm(ks[18], (DEPTH, d, D_FF), d ** -0.5),
        'w_up': nrm(ks[19], (DEPTH, d, D_FF), d ** -0.5),
        'w_down': nrm(ks[20], (DEPTH, D_FF, d), D_FF ** -0.5),
        'norm_final_w': 1.0 + nrm(ks[21], (d,), 0.01),
    }


def reference(x_prompt, x_sample, state_ret, state_mlstm_C, state_mlstm_n, state_mlstm_m, state_hgrn,
              norm_mix_w, w_in_ab, b_if_ab, ml_norm_w, w_out_ab, w_in_c, lb_logits, hg_norm_w, w_out_c,
              norm_ffn_w, w_gate, w_up, w_down, norm_final_w):
    f32 = jnp.float32
    weights = (norm_mix_w, w_in_ab, b_if_ab, ml_norm_w, w_out_ab, w_in_c, lb_logits, hg_norm_w, w_out_c,
               norm_ffn_w, w_gate, w_up, w_down, norm_final_w)
    b, t = x_prompt.shape[0], x_prompt.shape[1]
    pos_p = jnp.arange(t, dtype=f32)
    pos_s = PAST_LEN + jnp.arange(x_sample.shape[1], dtype=f32)
    ret0 = jnp.zeros((N_AB_LAYERS, b, RET_HEADS, RET_DK, RET_DV), f32)
    mc0 = jnp.zeros((N_AB_LAYERS, b, ML_HEADS, ML_DK, ML_DV), f32)
    mn0 = jnp.zeros((N_AB_LAYERS, b, ML_HEADS, ML_DK), f32)
    mm0 = jnp.zeros((N_AB_LAYERS, b, ML_HEADS), f32)
    hg0 = jnp.zeros((N_C_LAYERS, b, HG_HEADS, HG_DK, HG_DV), f32)
    y_prompt, ret_p, mc_p, mn_p, mm_p, hg_p = _trunk(x_prompt, pos_p, ret0, mc0, mn0, mm0, hg0, *weights)
    y_sample, ret_s, mc_s, mn_s, mm_s, hg_s = _trunk(
        x_sample, pos_s, state_ret, state_mlstm_C, state_mlstm_n, state_mlstm_m, state_hgrn, *weights)
    return (y_prompt, y_sample, ret_p, mc_p, mn_p, mm_p, hg_p, ret_s, mc_s, mn_s, mm_s, hg_s)
```

```python
import functools

import numpy as np
import jax
import jax.numpy as jnp
from jax import lax
from jax.experimental import pallas as pl
from jax.experimental.pallas import tpu as pltpu

F32 = jnp.float32
BF16 = jnp.bfloat16

EPS = 1e-6
ROPE_BASE = 10000.0
PAST_LEN = 16384

RET_HEADS = 8
ML_HEADS = 8
HEAD_DIM_AB = 256
HG_DIM = 128
RET_CHUNK = 128
ML_CHUNK = 128
HG_CHUNK = 32
GATE_LANES = 128

V7X_VMEM_LIMIT_BYTES = 56 * 1024 * 1024


def _cparams(semantics):
    return pltpu.CompilerParams(dimension_semantics=semantics,
                                vmem_limit_bytes=V7X_VMEM_LIMIT_BYTES)


def _rmsnorm_kernel(x_ref, w_ref, o_ref):
    x = x_ref[...]
    y = x * lax.rsqrt(jnp.mean(x * x, axis=-1, keepdims=True) + EPS)
    o_ref[...] = (y * w_ref[...]).astype(o_ref.dtype)


def _rmsnorm(x, w, out_dtype, tm=256):
    m, d = x.shape
    return pl.pallas_call(
        _rmsnorm_kernel,
        grid=(m // tm,),
        in_specs=[pl.BlockSpec((tm, d), lambda i: (i, 0)),
                  pl.BlockSpec((1, d), lambda i: (0, 0))],
        out_specs=pl.BlockSpec((tm, d), lambda i: (i, 0)),
        out_shape=jax.ShapeDtypeStruct((m, d), out_dtype),
        compiler_params=_cparams(("parallel",)),
        name="rmsnorm",
    )(x, w.reshape(1, d).astype(F32))


def _mm_kernel(x_ref, w_ref, o_ref):
    o_ref[...] = jnp.dot(x_ref[...], w_ref[...], preferred_element_type=F32).astype(o_ref.dtype)


def _mm_residual_kernel(x_ref, w_ref, r_ref, o_ref):
    o_ref[...] = r_ref[...] + jnp.dot(x_ref[...], w_ref[...], preferred_element_type=F32)


def _gate_up_kernel(x_ref, wg_ref, wu_ref, o_ref):
    x = x_ref[...]
    a = jnp.dot(x, wg_ref[...], preferred_element_type=F32)
    b = jnp.dot(x, wu_ref[...], preferred_element_type=F32)
    o_ref[...] = (jax.nn.silu(a) * b).astype(o_ref.dtype)


def _matmul(x, w, tm, tn, out_dtype=F32, name="matmul"):
    m, k = x.shape
    n = w.shape[1]
    return pl.pallas_call(
        _mm_kernel,
        grid=(m // tm, n // tn),
        in_specs=[pl.BlockSpec((tm, k), lambda i, j: (i, 0)),
                  pl.BlockSpec((k, tn), lambda i, j: (0, j))],
        out_specs=pl.BlockSpec((tm, tn), lambda i, j: (i, j)),
        out_shape=jax.ShapeDtypeStruct((m, n), out_dtype),
        compiler_params=_cparams(("parallel", "arbitrary")),
        name=name,
    )(x, w)


def _matmul_residual(x, w, res, tm, tn, single_buffer_lhs=False, name="matmul_residual"):
    m, k = x.shape
    n = w.shape[1]
    lhs_kwargs = {"pipeline_mode": pl.Buffered(1)} if single_buffer_lhs else {}
    return pl.pallas_call(
        _mm_residual_kernel,
        grid=(m // tm, n // tn),
        in_specs=[pl.BlockSpec((tm, k), lambda i, j: (i, 0), **lhs_kwargs),
                  pl.BlockSpec((k, tn), lambda i, j: (0, j)),
                  pl.BlockSpec((tm, tn), lambda i, j: (i, j))],
        out_specs=pl.BlockSpec((tm, tn), lambda i, j: (i, j)),
        out_shape=jax.ShapeDtypeStruct((m, n), F32),
        compiler_params=_cparams(("parallel", "arbitrary")),
        name=name,
    )(x, w, res)


def _gate_up(x, wg, wu, tm, tn):
    m, k = x.shape
    n = wg.shape[1]
    return pl.pallas_call(
        _gate_up_kernel,
        grid=(m // tm, n // tn),
        in_specs=[pl.BlockSpec((tm, k), lambda i, j: (i, 0)),
                  pl.BlockSpec((k, tn), lambda i, j: (0, j)),
                  pl.BlockSpec((k, tn), lambda i, j: (0, j))],
        out_specs=pl.BlockSpec((tm, tn), lambda i, j: (i, j)),
        out_shape=jax.ShapeDtypeStruct((m, n), BF16),
        compiler_params=_cparams(("parallel", "arbitrary")),
        name="gate_up",
    )(x, wg, wu)


def _dot_nt(a, b):
    return lax.dot_general(a, b, (((1,), (1,)), ((), ())), preferred_element_type=F32)


def _dot_tn(a, b):
    return lax.dot_general(a, b, (((0,), (0,)), ((), ())), preferred_element_type=F32)


def _dot(a, b):
    return jnp.dot(a, b, preferred_element_type=F32)


def _head_norm(x):
    return x * lax.rsqrt(jnp.mean(x * x, axis=-1, keepdims=True) + EPS)


def _col_to_row(col, eye):
    return jnp.sum(jnp.where(eye, col, 0.0), axis=0, keepdims=True)


def _row_to_col(row, eye):
    return jnp.sum(jnp.where(eye, row, 0.0), axis=1, keepdims=True)


def _rotary(x, cos, sin):
    half = x.shape[-1] // 2
    x1, x2 = x[:, :half], x[:, half:]
    return jnp.concatenate([x1 * cos - x2 * sin, x1 * sin + x2 * cos], axis=-1)


def _ret_kernel(L, units, carry, *refs):
    if carry:
        (q_ref, k_ref, v_ref, g_ref, cos_ref, sin_ref, dec_ref, qd_ref, kd_ref, cd_ref,
         _mixed_in, o_ref, sout_ref) = refs
        s0_ref = None
    else:
        (q_ref, k_ref, v_ref, g_ref, cos_ref, sin_ref, dec_ref, qd_ref, kd_ref, cd_ref,
         s0_ref, _mixed_in, o_ref, sout_ref) = refs

    if carry:
        @pl.when(pl.program_id(2) == 0)
        def _():
            sout_ref[...] = jnp.zeros_like(sout_ref)

    dec = dec_ref[...]
    qd = qd_ref[...]
    kd = kd_ref[...]
    cd = cd_ref[...]
    scale = HEAD_DIM_AB ** -0.5

    def unit(u, c):
        rows = pl.ds(pl.multiple_of(u * L, L), L)
        if carry:
            cos, sin = cos_ref[rows, :], sin_ref[rows, :]
            s = sout_ref[...]
        else:
            cos, sin = cos_ref[...], sin_ref[...]
            s = s0_ref[u]
        q = _rotary(q_ref[rows, :], cos, sin)
        k = _rotary(k_ref[rows, :], cos, sin) * scale
        v = v_ref[rows, :]
        att = _dot_nt(q, k) * dec
        o = _dot(att, v) + _dot(q, s) * qd
        s_new = cd * s + _dot_tn(k * kd, v)
        out = jax.nn.silu(g_ref[rows, :]) * _head_norm(o)
        o_ref[rows, :] = out.astype(o_ref.dtype)
        if carry:
            sout_ref[...] = s_new
        else:
            sout_ref[u] = s_new
        return c

    lax.fori_loop(0, units, unit, 0)


def _retention_tables(L):
    log_gamma = jnp.log1p(-jnp.exp2(-5.0 - jnp.arange(RET_HEADS, dtype=F32)))[:, None]
    idx = jnp.arange(L, dtype=F32)
    diff = idx[:, None] - idx[None, :]
    decay = jnp.exp(jnp.where(diff >= 0, log_gamma[:, :, None] * diff, -jnp.inf))
    q_decay = jnp.exp(log_gamma * (idx + 1.0))[:, :, None]
    k_decay = jnp.exp(log_gamma * (L - 1.0 - idx))[:, :, None]
    chunk_decay = jnp.exp(log_gamma[:, 0] * L)[:, None, None]
    return decay, q_decay, k_decay, chunk_decay


def _rope_tables(pos):
    half = HEAD_DIM_AB // 2
    inv_freq = ROPE_BASE ** (-jnp.arange(half, dtype=F32) / half)
    ang = pos[:, None] * inv_freq[None, :]
    return jnp.cos(ang), jnp.sin(ang)


def _retention(proj, mixed, *, row0, batch, seq, L, rows_per_step, pos0, s0=None):
    d = HEAD_DIM_AB
    h_ = RET_HEADS
    carry = s0 is None
    units = rows_per_step // L
    decay, q_decay, k_decay, chunk_decay = _retention_tables(L)
    cos, sin = _rope_tables(pos0 + jnp.arange(seq, dtype=F32))
    rb0 = row0 // rows_per_step
    if carry:
        tblocks = seq // rows_per_step
        grid = (batch, h_, tblocks)
        row_idx = lambda b, h, t: rb0 + b * tblocks + t
        rope_spec = pl.BlockSpec((rows_per_step, d // 2), lambda b, h, t: (t, 0))
        sout_spec = pl.BlockSpec((None, None, d, d), lambda b, h, t: (b, h, 0, 0))
    else:
        nb = rows_per_step // seq
        grid = (batch // nb, h_, 1)
        row_idx = lambda b, h, t: rb0 + b
        rope_spec = pl.BlockSpec((seq, d // 2), lambda b, h, t: (0, 0))
        sout_spec = pl.BlockSpec((nb, None, d, d), lambda b, h, t: (b, h, 0, 0))

    def col_spec(c0):
        return pl.BlockSpec((rows_per_step, d), lambda b, h, t: (row_idx(b, h, t), c0 + h))

    in_specs = [col_spec(0), col_spec(h_), col_spec(2 * h_), col_spec(3 * h_),
                rope_spec, rope_spec,
                pl.BlockSpec((None, L, L), lambda b, h, t: (h, 0, 0)),
                pl.BlockSpec((None, L, 1), lambda b, h, t: (h, 0, 0)),
                pl.BlockSpec((None, L, 1), lambda b, h, t: (h, 0, 0)),
                pl.BlockSpec((None, 1, 1), lambda b, h, t: (h, 0, 0))]
    args = [proj, proj, proj, proj, cos, sin, decay, q_decay, k_decay, chunk_decay]
    if not carry:
        in_specs.append(sout_spec)
        args.append(s0)
    in_specs.append(pl.BlockSpec(memory_space=pl.ANY))
    args.append(mixed)
    return pl.pallas_call(
        functools.partial(_ret_kernel, L, units, carry),
        grid=grid,
        in_specs=in_specs,
        out_specs=[col_spec(0), sout_spec],
        out_shape=[jax.ShapeDtypeStruct(mixed.shape, mixed.dtype),
                   jax.ShapeDtypeStruct((batch, h_, d, d), F32)],
        input_output_aliases={len(args) - 1: 0},
        compiler_params=_cparams(("parallel", "parallel", "arbitrary")),
        name="retention_prompt" if carry else "retention_sample",
    )(*args)


def _mlstm_kernel(L, units, carry, *refs):
    if carry:
        (q_ref, k_ref, v_ref, og_ref, gates_ref, bias_ref, nw_ref,
         _mixed_in, o_ref, c_ref, n_ref, m_ref) = refs
        c0_ref = n0_ref = m0_ref = None
    else:
        (q_ref, k_ref, v_ref, og_ref, gates_ref, bias_ref, nw_ref, c0_ref, n0_ref, m0_ref,
         _mixed_in, o_ref, c_ref, n_ref, m_ref) = refs

    if carry:
        @pl.when(pl.program_id(2) == 0)
        def _():
            c_ref[...] = jnp.zeros_like(c_ref)
            n_ref[...] = jnp.zeros_like(n_ref)
            m_ref[...] = jnp.zeros_like(m_ref)

    head = pl.program_id(1)
    bias = bias_ref[...]
    nw = nw_ref[...]
    lane = lax.broadcasted_iota(jnp.int32, (L, GATE_LANES), 1)
    ti = lax.broadcasted_iota(jnp.int32, (L, L), 0)
    si = lax.broadcasted_iota(jnp.int32, (L, L), 1)
    eye = ti == si
    causal = si <= ti
    scale = HEAD_DIM_AB ** -0.5

    def unit(u, carry_val):
        rows = pl.ds(pl.multiple_of(u * L, L), L)
        if carry:
            c, n, m = c_ref[...], n_ref[...], m_ref[:, 0:1]
        else:
            c, n, m = c0_ref[u], n0_ref[u], m0_ref[u][:, 0:1]
        x = gates_ref[rows, :] + bias
        ig = jnp.sum(jnp.where(lane == head, x, 0.0), axis=1, keepdims=True)
        fpre = jnp.sum(jnp.where(lane == head + ML_HEADS, x, 0.0), axis=1, keepdims=True)
        lf = jnp.minimum(fpre, 0.0) - jnp.log1p(jnp.exp(-jnp.abs(fpre)))
        f_cum = jnp.sum(jnp.where(causal, _col_to_row(lf, eye), 0.0), axis=1, keepdims=True)
        a = ig - f_cum
        a_row = _col_to_row(a, eye)
        cmax = jnp.max(jnp.where(causal, a_row, -jnp.inf), axis=1, keepdims=True)
        m_t = f_cum + jnp.maximum(m, cmax)
        d_mat = jnp.exp(jnp.where(causal, (f_cum - m_t) + a_row, -jnp.inf))
        q = q_ref[rows, :]
        k = k_ref[rows, :] * scale
        v = v_ref[rows, :]
        s = _dot_nt(q, k) * d_mat
        inter = jnp.exp(f_cum + m - m_t)
        num = inter * _dot(q, c) + _dot(s, v)
        den = inter * jnp.sum(q * n, axis=1, keepdims=True) + jnp.sum(s, axis=1, keepdims=True)
        hh = num / jnp.maximum(jnp.abs(den), jnp.exp(-m_t))
        f_end = f_cum[L - 1:L, :]
        m_end = m_t[L - 1:L, :]
        w = jnp.exp(f_end - m_end + a)
        cdec = jnp.exp(f_end + m - m_end)
        kw = k * w
        c_new = cdec * c + _dot_tn(kw, v)
        n_new = cdec * n + jnp.sum(kw, axis=0, keepdims=True)
        out = jax.nn.sigmoid(og_ref[rows, :]) * (_head_norm(hh) * nw)
        o_ref[rows, :] = out.astype(o_ref.dtype)
        m_new = jnp.broadcast_to(m_end, (1, GATE_LANES))
        if carry:
            c_ref[...] = c_new
            n_ref[...] = n_new
            m_ref[...] = m_new
        else:
            c_ref[u] = c_new
            n_ref[u] = n_new
            m_ref[u] = m_new
        return carry_val

    lax.fori_loop(0, units, unit, 0)


def _mlstm(proj, gates, bias, norm_w, mixed, *, row0, batch, seq, L, rows_per_step,
           c0=None, n0=None, m0=None):
    d = HEAD_DIM_AB
    h_ = ML_HEADS
    carry = c0 is None
    units = rows_per_step // L
    rb0 = row0 // rows_per_step
    col0 = 4 * RET_HEADS
    if carry:
        tblocks = seq // rows_per_step
        grid = (batch, h_, tblocks)
        row_idx = lambda b, h, t: rb0 + b * tblocks + t
        lead = None
    else:
        nb = rows_per_step // seq
        grid = (batch // nb, h_, 1)
        row_idx = lambda b, h, t: rb0 + b
        lead = nb
    c_spec = pl.BlockSpec((lead, None, d, d), lambda b, h, t: (b, h, 0, 0))
    n_spec = pl.BlockSpec((lead, None, 1, d), lambda b, h, t: (b, h, 0, 0))
    m_spec = pl.BlockSpec((lead, None, 1, GATE_LANES), lambda b, h, t: (b, h, 0, 0))

    def col_spec(c_blk):
        return pl.BlockSpec((rows_per_step, d), lambda b, h, t: (row_idx(b, h, t), c_blk + h))

    in_specs = [col_spec(col0), col_spec(col0 + h_), col_spec(col0 + 2 * h_), col_spec(col0 + 3 * h_),
                pl.BlockSpec((rows_per_step, GATE_LANES), lambda b, h, t: (row_idx(b, h, t), 0)),
                pl.BlockSpec((1, GATE_LANES), lambda b, h, t: (0, 0)),
                pl.BlockSpec((1, d), lambda b, h, t: (0, h))]
    args = [proj, proj, proj, proj, gates, bias, norm_w]
    if not carry:
        in_specs += [c_spec, n_spec, m_spec]
        args += [c0, n0, m0]
    in_specs.append(pl.BlockSpec(memory_space=pl.ANY))
    args.append(mixed)
    return pl.pallas_call(
        functools.partial(_mlstm_kernel, L, units, carry),
        grid=grid,
        in_specs=in_specs,
        out_specs=[col_spec(RET_HEADS), c_spec, n_spec, m_spec],
        out_shape=[jax.ShapeDtypeStruct(mixed.shape, mixed.dtype),
                   jax.ShapeDtypeStruct((batch, h_, d, d), F32),
                   jax.ShapeDtypeStruct((batch, h_, 1, d), F32),
                   jax.ShapeDtypeStruct((batch, h_, 1, GATE_LANES), F32)],
        input_output_aliases={len(args) - 1: 0},
        compiler_params=_cparams(("parallel", "parallel", "arbitrary")),
        name="mlstm_prompt" if carry else "mlstm_sample",
    )(*args)


def _hgrn_kernel(L, units, carry, *refs):
    if carry:
        (q_ref, f_ref, i_ref, g_ref, lbl_ref, nw_ref, _mixed_in, o_ref, sout_ref) = refs
        s0_ref = None
    else:
        (q_ref, f_ref, i_ref, g_ref, lbl_ref, nw_ref, s0_ref, _mixed_in, o_ref, sout_ref) = refs

    if carry:
        @pl.when(pl.program_id(2) == 0)
        def _():
            sout_ref[...] = jnp.zeros_like(sout_ref)

    d = HG_DIM
    lg = lbl_ref[...]
    e = jnp.exp(lg - jnp.max(lg, axis=0, keepdims=True))
    sm = e / jnp.sum(e, axis=0, keepdims=True)
    lb = (sm[0:1, :] + sm[1:2, :]) - sm[0:1, :]
    nw = nw_ref[...]
    groups = L // 8
    row8 = lax.broadcasted_iota(jnp.int32, (8, d), 0)
    ti = lax.broadcasted_iota(jnp.int32, (d, d), 0)
    si = lax.broadcasted_iota(jnp.int32, (d, d), 1)
    eye_d = ti == si

    def unit(u, carry_val):
        rows = pl.ds(pl.multiple_of(u * L, L), L)
        s = sout_ref[...] if carry else s0_ref[u]
        q = jax.nn.silu(q_ref[rows, :])
        fg = lb + (1.0 - lb) * jax.nn.sigmoid(f_ref[rows, :])
        k = 1.0 - fg
        logf = jnp.log(fg)
        v = i_ref[rows, :]
        gg = []
        base = jnp.zeros((1, d), F32)
        for j in range(groups):
            x = logf[8 * j:8 * j + 8, :]
            cum = jnp.zeros((8, d), F32)
            for r in range(8):
                cum = cum + jnp.where(row8 >= r, x[r:r + 1, :], 0.0)
            cum = cum + base
            gg.append(cum)
            base = cum[7:8, :]
        g_cum = jnp.concatenate(gg, axis=0)
        qg = [q[8 * j:8 * j + 8, :] for j in range(groups)]
        acc = [jnp.zeros((8, d), F32) for _ in range(groups)]
        for r in range(L):
            k_r = k[r:r + 1, :]
            v_r = v[r:r + 1, :]
            g_r = gg[r // 8][r % 8:r % 8 + 1, :]
            for j in range(r // 8, groups):
                diff = gg[j] - g_r
                if j == r // 8:
                    diff = jnp.where(row8 >= (r - 8 * j), diff, -jnp.inf)
                att = jnp.sum(qg[j] * k_r * jnp.exp(diff), axis=1, keepdims=True)
                acc[j] = acc[j] + att * v_r
        o = _dot(q * jnp.exp(g_cum), s) + jnp.concatenate(acc, axis=0)
        g_end = base
        eg_col = _row_to_col(jnp.exp(g_end), eye_d)
        s_new = eg_col * s + _dot_tn(k * jnp.exp(g_end - g_cum), v)
        out = (_head_norm(o) * nw) * jax.nn.silu(g_ref[rows, :])
        o_ref[rows, :] = out.astype(o_ref.dtype)
        if carry:
            sout_ref[...] = s_new
        else:
            sout_ref[u] = s_new
        return carry_val

    lax.fori_loop(0, units, unit, 0)


def _hgrn(proj, lb_logits, norm_w, mixed, *, row0, batch, seq, L, rows_per_step, s0=None):
    d = HG_DIM
    h_ = proj.shape[1] // (4 * d)
    carry = s0 is None
    units = rows_per_step // L
    rb0 = row0 // rows_per_step
    if carry:
        tblocks = seq // rows_per_step
        grid = (batch, h_, tblocks)
        row_idx = lambda b, h, t: rb0 + b * tblocks + t
        sout_spec = pl.BlockSpec((None, None, d, d), lambda b, h, t: (b, h, 0, 0))
    else:
        nb = rows_per_step // seq
        grid = (batch // nb, h_, 1)
        row_idx = lambda b, h, t: rb0 + b
        sout_spec = pl.BlockSpec((nb, None, d, d), lambda b, h, t: (b, h, 0, 0))

    def col_spec(c0):
        return pl.BlockSpec((rows_per_step, d), lambda b, h, t: (row_idx(b, h, t), c0 + h))

    in_specs = [col_spec(0), col_spec(h_), col_spec(2 * h_), col_spec(3 * h_),
                pl.BlockSpec((2, d), lambda b, h, t: (0, h)),
                pl.BlockSpec((1, d), lambda b, h, t: (0, 0))]
    args = [proj, proj, proj, proj, lb_logits, norm_w]
    if not carry:
        in_specs.append(sout_spec)
        args.append(s0)
    in_specs.append(pl.BlockSpec(memory_space=pl.ANY))
    args.append(mixed)
    return pl.pallas_call(
        functools.partial(_hgrn_kernel, L, units, carry),
        grid=grid,
        in_specs=in_specs,
        out_specs=[col_spec(0), sout_spec],
        out_shape=[jax.ShapeDtypeStruct(mixed.shape, mixed.dtype),
                   jax.ShapeDtypeStruct((batch, h_, d, d), F32)],
        input_output_aliases={len(args) - 1: 0},
        compiler_params=_cparams(("parallel", "parallel", "arbitrary")),
        name="hgrn_prompt" if carry else "hgrn_sample",
    )(*args)


def _chunk_len(t, c):
    return c if t % c == 0 else t


def _swiglu_residual(x, norm_w, w_gate, w_up, w_down, tm):
    hn = _rmsnorm(x, norm_w, BF16)
    act = _gate_up(hn, w_gate.astype(BF16), w_up.astype(BF16), tm, 256)
    return _matmul_residual(act, w_down.astype(BF16), x, tm, 256, single_buffer_lhs=True,
                            name="down_proj")


def kernel(x_prompt, x_sample, state_ret, state_mlstm_C, state_mlstm_n, state_mlstm_m, state_hgrn,
           norm_mix_w, w_in_ab, b_if_ab, ml_norm_w, w_out_ab, w_in_c, lb_logits, hg_norm_w, w_out_c,
           norm_ffn_w, w_gate, w_up, w_down, norm_final_w):
    bp, tp, d = x_prompt.shape
    bs, ts, _ = x_sample.shape
    mp, ms = bp * tp, bs * ts
    tm = 1024
    x = jnp.concatenate([x_prompt.reshape(mp, d), x_sample.reshape(ms, d)], axis=0)

    ab_main = 4 * RET_HEADS * HEAD_DIM_AB + 4 * ML_HEADS * HEAD_DIM_AB
    hn = _rmsnorm(x, norm_mix_w[0], BF16)
    w_in = w_in_ab[0]
    proj = _matmul(hn, w_in[:, :ab_main].astype(BF16), tm, 512, name="in_proj_ab")
    w_gates = jnp.pad(w_in[:, ab_main:], ((0, 0), (0, GATE_LANES - 2 * ML_HEADS))).astype(BF16)
    gates = _matmul(hn, w_gates, tm, GATE_LANES, name="in_proj_gates")
    bias = jnp.pad(b_if_ab[0].astype(F32), (0, GATE_LANES - 2 * ML_HEADS)).reshape(1, GATE_LANES)
    ml_w = ml_norm_w[0].astype(F32).reshape(1, -1)

    mixed = jnp.zeros((mp + ms, d), BF16)
    lp = _chunk_len(tp, RET_CHUNK)
    ls = _chunk_len(ts, RET_CHUNK)
    mixed, ret_p = _retention(proj, mixed, row0=0, batch=bp, seq=tp, L=lp, rows_per_step=4 * lp,
                              pos0=0.0)
    mixed, ret_s = _retention(proj, mixed, row0=mp, batch=bs, seq=ts, L=ls, rows_per_step=4 * ls,
                              pos0=float(PAST_LEN), s0=state_ret[0])
    lp = _chunk_len(tp, ML_CHUNK)
    ls = _chunk_len(ts, ML_CHUNK)
    mixed, mc_p, mn_p, mm_p = _mlstm(proj, gates, bias, ml_w, mixed, row0=0, batch=bp, seq=tp, L=lp,
                                     rows_per_step=4 * lp)
    m0 = jnp.broadcast_to(state_mlstm_m[0][:, :, None, None], (bs, ML_HEADS, 1, GATE_LANES))
    mixed, mc_s, mn_s, mm_s = _mlstm(proj, gates, bias, ml_w, mixed, row0=mp, batch=bs, seq=ts, L=ls,
                                     rows_per_step=4 * ls, c0=state_mlstm_C[0],
                                     n0=state_mlstm_n[0][:, :, None, :], m0=m0)
    x = _matmul_residual(mixed, w_out_ab[0].astype(BF16), x, tm, 512, name="out_proj_ab")
    x = _swiglu_residual(x, norm_ffn_w[0], w_gate[0], w_up[0], w_down[0], tm)

    hn = _rmsnorm(x, norm_mix_w[1], BF16)
    proj = _matmul(hn, w_in_c[0].astype(BF16), tm, 512, name="in_proj_c")
    mixed = jnp.zeros((mp + ms, d), BF16)
    hg_w = hg_norm_w[0].astype(F32).reshape(1, -1)
    lbl = lb_logits.astype(F32)
    lp = _chunk_len(tp, HG_CHUNK)
    ls = _chunk_len(ts, HG_CHUNK)
    mixed, hg_p = _hgrn(proj, lbl, hg_w, mixed, row0=0, batch=bp, seq=tp, L=lp, rows_per_step=16 * lp)
    mixed, hg_s = _hgrn(proj, lbl, hg_w, mixed, row0=mp, batch=bs, seq=ts, L=ls, rows_per_step=16 * ls,
                        s0=state_hgrn[0])
    x = _matmul_residual(mixed, w_out_c[0].astype(BF16), x, tm, 512, name="out_proj_c")
    x = _swiglu_residual(x, norm_ffn_w[1], w_gate[1], w_up[1], w_down[1], tm)

    y = _rmsnorm(x, norm_final_w, F32)
    y_prompt = y[:mp].reshape(bp, tp, d)
    y_sample = y[mp:].reshape(bs, ts, d)
    return (y_prompt, y_sample,
            ret_p[None], mc_p[None], mn_p[:, :, 0, :][None], mm_p[:, :, 0, 0][None], hg_p[None],
            ret_s[None], mc_s[None], mn_s[:, :, 0, :][None], mm_s[:, :, 0, 0][None], hg_s[None])
```

```python
import functools

import numpy as np
import jax
import jax.numpy as jnp
from jax import lax
from jax.experimental import pallas as pl
from jax.experimental.pallas import tpu as pltpu

F32 = jnp.float32
BF16 = jnp.bfloat16

EPS = 1e-6
ROPE_BASE = 10000.0
PAST_LEN = 16384

RET_HEADS = 8
ML_HEADS = 8
HEAD_DIM_AB = 256
HG_DIM = 128
RET_CHUNK = 128
ML_CHUNK = 128
HG_CHUNK = 32
GATE_LANES = 128

V7X_VMEM_LIMIT_BYTES = 56 * 1024 * 1024

UNROLL_AB_PROMPT = 2
UNROLL_AB_SAMPLE = 4
UNROLL_HG_PROMPT = 16
UNROLL_HG_SAMPLE = 8


def _cparams(semantics):
    return pltpu.CompilerParams(dimension_semantics=semantics,
                                vmem_limit_bytes=V7X_VMEM_LIMIT_BYTES)


def _rmsnorm_kernel(x_ref, w_ref, o_ref):
    x = x_ref[...]
    y = x * lax.rsqrt(jnp.mean(x * x, axis=-1, keepdims=True) + EPS)
    o_ref[...] = (y * w_ref[...]).astype(o_ref.dtype)


def _rmsnorm(x, w, out_dtype, tm=512, row0=0, rows=None):
    d = x.shape[1]
    rows = x.shape[0] if rows is None else rows
    rb0 = row0 // tm
    return pl.pallas_call(
        _rmsnorm_kernel,
        grid=(rows // tm,),
        in_specs=[pl.BlockSpec((tm, d), lambda i: (rb0 + i, 0)),
                  pl.BlockSpec((1, d), lambda i: (0, 0))],
        out_specs=pl.BlockSpec((tm, d), lambda i: (i, 0)),
        out_shape=jax.ShapeDtypeStruct((rows, d), out_dtype),
        compiler_params=_cparams(("parallel",)),
        name="rmsnorm",
    )(x, w.reshape(1, d).astype(F32))


def _mm_kernel(x_ref, w_ref, o_ref):
    o_ref[...] = jnp.dot(x_ref[...], w_ref[...], preferred_element_type=F32).astype(o_ref.dtype)


def _mm_residual_kernel(x_ref, w_ref, r_ref, o_ref):
    o_ref[...] = r_ref[...] + jnp.dot(x_ref[...], w_ref[...], preferred_element_type=F32)


def _gate_up_kernel(x_ref, wg_ref, wu_ref, o_ref):
    x = x_ref[...]
    a = jnp.dot(x, wg_ref[...], preferred_element_type=F32)
    b = jnp.dot(x, wu_ref[...], preferred_element_type=F32)
    o_ref[...] = (jax.nn.silu(a) * b).astype(o_ref.dtype)


def _weight_spec(w, layer, tn):
    return pl.BlockSpec((None, w.shape[1], tn), lambda i, j: (layer, 0, j))


def _matmul(x, w, layer, tm, tn, n_out=None, out_dtype=F32, name="matmul"):
    m, k = x.shape
    n = w.shape[2] if n_out is None else n_out
    return pl.pallas_call(
        _mm_kernel,
        grid=(m // tm, n // tn),
        in_specs=[pl.BlockSpec((tm, k), lambda i, j: (i, 0)),
                  _weight_spec(w, layer, tn)],
        out_specs=pl.BlockSpec((tm, tn), lambda i, j: (i, j)),
        out_shape=jax.ShapeDtypeStruct((m, n), out_dtype),
        compiler_params=_cparams(("parallel", "arbitrary")),
        name=name,
    )(x, w)


def _matmul_residual(x, w, layer, res, tm, tn, single_buffer_lhs=False, name="matmul_residual"):
    m, k = x.shape
    n = w.shape[2]
    lhs_kwargs = {"pipeline_mode": pl.Buffered(1)} if single_buffer_lhs else {}
    return pl.pallas_call(
        _mm_residual_kernel,
        grid=(m // tm, n // tn),
        in_specs=[pl.BlockSpec((tm, k), lambda i, j: (i, 0), **lhs_kwargs),
                  _weight_spec(w, layer, tn),
                  pl.BlockSpec((tm, tn), lambda i, j: (i, j))],
        out_specs=pl.BlockSpec((tm, tn), lambda i, j: (i, j)),
        out_shape=jax.ShapeDtypeStruct((m, n), F32),
        compiler_params=_cparams(("parallel", "arbitrary")),
        name=name,
    )(x, w, res)


def _gate_up(x, wg, wu, layer, tm, tn):
    m, k = x.shape
    n = wg.shape[2]
    return pl.pallas_call(
        _gate_up_kernel,
        grid=(m // tm, n // tn),
        in_specs=[pl.BlockSpec((tm, k), lambda i, j: (i, 0)),
                  _weight_spec(wg, layer, tn),
                  _weight_spec(wu, layer, tn)],
        out_specs=pl.BlockSpec((tm, tn), lambda i, j: (i, j)),
        out_shape=jax.ShapeDtypeStruct((m, n), BF16),
        compiler_params=_cparams(("parallel", "arbitrary")),
        name="gate_up",
    )(x, wg, wu)


def _dot_nt(a, b):
    return lax.dot_general(a, b, (((1,), (1,)), ((), ())), preferred_element_type=F32)


def _dot_tn(a, b):
    return lax.dot_general(a, b, (((0,), (0,)), ((), ())), preferred_element_type=F32)


def _dot(a, b):
    return jnp.dot(a, b, preferred_element_type=F32)


def _head_norm(x):
    return x * lax.rsqrt(jnp.mean(x * x, axis=-1, keepdims=True) + EPS)


def _col_to_row(col, eye):
    return jnp.sum(jnp.where(eye, col, 0.0), axis=0, keepdims=True)


def _row_to_col(row, eye):
    return jnp.sum(jnp.where(eye, row, 0.0), axis=1, keepdims=True)


def _rotary(x, cos, sin):
    half = x.shape[-1] // 2
    x1, x2 = x[:, :half], x[:, half:]
    return jnp.concatenate([x1 * cos - x2 * sin, x1 * sin + x2 * cos], axis=-1)


def _ret_kernel(L, units, carry, unroll, *refs):
    if carry:
        (q_ref, k_ref, v_ref, g_ref, cos_ref, sin_ref, dec_ref, qd_ref, kd_ref, cd_ref,
         _mixed_in, o_ref, sout_ref) = refs
        s0_ref = None
    else:
        (q_ref, k_ref, v_ref, g_ref, cos_ref, sin_ref, dec_ref, qd_ref, kd_ref, cd_ref,
         s0_ref, _mixed_in, o_ref, sout_ref) = refs

    if carry:
        @pl.when(pl.program_id(2) == 0)
        def _():
            sout_ref[...] = jnp.zeros_like(sout_ref)

    dec = dec_ref[...]
    qd = qd_ref[...]
    kd = kd_ref[...]
    cd = cd_ref[...]
    scale = HEAD_DIM_AB ** -0.5

    def unit(u, c):
        rows = pl.ds(pl.multiple_of(u * L, L), L)
        if carry:
            cos, sin = cos_ref[rows, :], sin_ref[rows, :]
            s = sout_ref[...]
        else:
            cos, sin = cos_ref[...], sin_ref[...]
            s = s0_ref[u]
        q = _rotary(q_ref[rows, :], cos, sin)
        k = _rotary(k_ref[rows, :], cos, sin) * scale
        v = v_ref[rows, :]
        att = _dot_nt(q, k) * dec
        o = _dot(att, v) + _dot(q, s) * qd
        s_new = cd * s + _dot_tn(k * kd, v)
        out = jax.nn.silu(g_ref[rows, :]) * _head_norm(o)
        o_ref[rows, :] = out.astype(o_ref.dtype)
        if carry:
            sout_ref[...] = s_new
        else:
            sout_ref[u] = s_new
        return c

    lax.fori_loop(0, units, unit, 0, unroll=unroll)


def _retention_tables(L):
    log_gamma = jnp.log1p(-jnp.exp2(-5.0 - jnp.arange(RET_HEADS, dtype=F32)))[:, None]
    idx = jnp.arange(L, dtype=F32)
    diff = idx[:, None] - idx[None, :]
    decay = jnp.exp(jnp.where(diff >= 0, log_gamma[:, :, None] * diff, -jnp.inf))
    q_decay = jnp.exp(log_gamma * (idx + 1.0))[:, :, None]
    k_decay = jnp.exp(log_gamma * (L - 1.0 - idx))[:, :, None]
    chunk_decay = jnp.exp(log_gamma[:, 0] * L)[:, None, None]
    return decay, q_decay, k_decay, chunk_decay


def _rope_tables(pos):
    half = HEAD_DIM_AB // 2
    inv_freq = ROPE_BASE ** (-jnp.arange(half, dtype=F32) / half)
    ang = pos[:, None] * inv_freq[None, :]
    return jnp.cos(ang), jnp.sin(ang)


def _retention(proj, mixed, *, row0, batch, seq, L, rows_per_step, pos0, s0=None, unroll=1):
    d = HEAD_DIM_AB
    h_ = RET_HEADS
    carry = s0 is None
    units = rows_per_step // L
    decay, q_decay, k_decay, chunk_decay = _retention_tables(L)
    cos, sin = _rope_tables(pos0 + jnp.arange(seq, dtype=F32))
    rb0 = row0 // rows_per_step
    if carry:
        tblocks = seq // rows_per_step
        grid = (batch, h_, tblocks)
        row_idx = lambda b, h, t: rb0 + b * tblocks + t
        rope_spec = pl.BlockSpec((rows_per_step, d // 2), lambda b, h, t: (t, 0))
        sout_spec = pl.BlockSpec((None, None, d, d), lambda b, h, t: (b, h, 0, 0))
    else:
        nb = rows_per_step // seq
        grid = (batch // nb, h_, 1)
        row_idx = lambda b, h, t: rb0 + b
        rope_spec = pl.BlockSpec((seq, d // 2), lambda b, h, t: (0, 0))
        sout_spec = pl.BlockSpec((nb, None, d, d), lambda b, h, t: (b, h, 0, 0))

    def col_spec(c0):
        return pl.BlockSpec((rows_per_step, d), lambda b, h, t: (row_idx(b, h, t), c0 + h))

    in_specs = [col_spec(0), col_spec(h_), col_spec(2 * h_), col_spec(3 * h_),
                rope_spec, rope_spec,
                pl.BlockSpec((None, L, L), lambda b, h, t: (h, 0, 0)),
                pl.BlockSpec((None, L, 1), lambda b, h, t: (h, 0, 0)),
                pl.BlockSpec((None, L, 1), lambda b, h, t: (h, 0, 0)),
                pl.BlockSpec((None, 1, 1), lambda b, h, t: (h, 0, 0))]
    args = [proj, proj, proj, proj, cos, sin, decay, q_decay, k_decay, chunk_decay]
    if not carry:
        in_specs.append(sout_spec)
        args.append(s0)
    in_specs.append(pl.BlockSpec(memory_space=pl.ANY))
    args.append(mixed)
    return pl.pallas_call(
        functools.partial(_ret_kernel, L, units, carry, unroll),
        grid=grid,
        in_specs=in_specs,
        out_specs=[col_spec(0), sout_spec],
        out_shape=[jax.ShapeDtypeStruct(mixed.shape, mixed.dtype),
                   jax.ShapeDtypeStruct((batch, h_, d, d), F32)],
        input_output_aliases={len(args) - 1: 0},
        compiler_params=_cparams(("parallel", "parallel", "arbitrary")),
        name="retention_prompt" if carry else "retention_sample",
    )(*args)


def _mlstm_kernel(L, units, carry, unroll, *refs):
    if carry:
        (q_ref, k_ref, v_ref, og_ref, gates_ref, bias_ref, nw_ref,
         _mixed_in, o_ref, c_ref, n_ref, m_ref) = refs
        c0_ref = n0_ref = m0_ref = None
    else:
        (q_ref, k_ref, v_ref, og_ref, gates_ref, bias_ref, nw_ref, c0_ref, n0_ref, m0_ref,
         _mixed_in, o_ref, c_ref, n_ref, m_ref) = refs

    if carry:
        @pl.when(pl.program_id(2) == 0)
        def _():
            c_ref[...] = jnp.zeros_like(c_ref)
            n_ref[...] = jnp.zeros_like(n_ref)
            m_ref[...] = jnp.zeros_like(m_ref)

    head = pl.program_id(1)
    bias = bias_ref[...]
    nw = nw_ref[...]
    lane = lax.broadcasted_iota(jnp.int32, (L, GATE_LANES), 1)
    ti = lax.broadcasted_iota(jnp.int32, (L, L), 0)
    si = lax.broadcasted_iota(jnp.int32, (L, L), 1)
    eye = ti == si
    causal = si <= ti
    scale = HEAD_DIM_AB ** -0.5

    def unit(u, carry_val):
        rows = pl.ds(pl.multiple_of(u * L, L), L)
        if carry:
            c, n, m = c_ref[...], n_ref[...], m_ref[:, 0:1]
        else:
            c, n, m = c0_ref[u], n0_ref[u], m0_ref[u][:, 0:1]
        x = gates_ref[rows, :] + bias
        ig = jnp.sum(jnp.where(lane == head, x, 0.0), axis=1, keepdims=True)
        fpre = jnp.sum(jnp.where(lane == head + ML_HEADS, x, 0.0), axis=1, keepdims=True)
        lf = jnp.minimum(fpre, 0.0) - jnp.log1p(jnp.exp(-jnp.abs(fpre)))
        f_cum = jnp.sum(jnp.where(causal, _col_to_row(lf, eye), 0.0), axis=1, keepdims=True)
        a = ig - f_cum
        a_row = _col_to_row(a, eye)
        cmax = jnp.max(jnp.where(causal, a_row, -jnp.inf), axis=1, keepdims=True)
        m_t = f_cum + jnp.maximum(m, cmax)
        d_mat = jnp.exp(jnp.where(causal, (f_cum - m_t) + a_row, -jnp.inf))
        q = q_ref[rows, :]
        k = k_ref[rows, :] * scale
        v = v_ref[rows, :]
        s = _dot_nt(q, k) * d_mat
        inter = jnp.exp(f_cum + m - m_t)
        num = inter * _dot(q, c) + _dot(s, v)
        den = inter * jnp.sum(q * n, axis=1, keepdims=True) + jnp.sum(s, axis=1, keepdims=True)
        hh = num * (1.0 / jnp.maximum(jnp.abs(den), jnp.exp(-m_t)))
        f_end = f_cum[L - 1:L, :]
        m_end = m_t[L - 1:L, :]
        w = jnp.exp(f_end - m_end + a)
        cdec = jnp.exp(f_end + m - m_end)
        kw = k * w
        c_new = cdec * c + _dot_tn(kw, v)
        n_new = cdec * n + jnp.sum(kw, axis=0, keepdims=True)
        out = jax.nn.sigmoid(og_ref[rows, :]) * (_head_norm(hh) * nw)
        o_ref[rows, :] = out.astype(o_ref.dtype)
        m_new = jnp.broadcast_to(m_end, (1, GATE_LANES))
        if carry:
            c_ref[...] = c_new
            n_ref[...] = n_new
            m_ref[...] = m_new
        else:
            c_ref[u] = c_new
            n_ref[u] = n_new
            m_ref[u] = m_new
        return carry_val

    lax.fori_loop(0, units, unit, 0, unroll=unroll)


def _mlstm(proj, gates, bias, norm_w, mixed, *, row0, batch, seq, L, rows_per_step,
           c0=None, n0=None, m0=None, unroll=1):
    d = HEAD_DIM_AB
    h_ = ML_HEADS
    carry = c0 is None
    units = rows_per_step // L
    rb0 = row0 // rows_per_step
    col0 = 4 * RET_HEADS
    if carry:
        tblocks = seq // rows_per_step
        grid = (batch, h_, tblocks)
        row_idx = lambda b, h, t: rb0 + b * tblocks + t
        lead = None
    else:
        nb = rows_per_step // seq
        grid = (batch // nb, h_, 1)
        row_idx = lambda b, h, t: rb0 + b
        lead = nb
    c_spec = pl.BlockSpec((lead, None, d, d), lambda b, h, t: (b, h, 0, 0))
    n_spec = pl.BlockSpec((lead, None, 1, d), lambda b, h, t: (b, h, 0, 0))
    m_spec = pl.BlockSpec((lead, None, 1, GATE_LANES), lambda b, h, t: (b, h, 0, 0))

    def col_spec(c_blk):
        return pl.BlockSpec((rows_per_step, d), lambda b, h, t: (row_idx(b, h, t), c_blk + h))

    in_specs = [col_spec(col0), col_spec(col0 + h_), col_spec(col0 + 2 * h_), col_spec(col0 + 3 * h_),
                pl.BlockSpec((rows_per_step, GATE_LANES), lambda b, h, t: (row_idx(b, h, t), 0)),
                pl.BlockSpec((1, GATE_LANES), lambda b, h, t: (0, 0)),
                pl.BlockSpec((1, d), lambda b, h, t: (0, h))]
    args = [proj, proj, proj, proj, gates, bias, norm_w]
    if not carry:
        in_specs += [c_spec, n_spec, m_spec]
        args += [c0, n0, m0]
    in_specs.append(pl.BlockSpec(memory_space=pl.ANY))
    args.append(mixed)
    return pl.pallas_call(
        functools.partial(_mlstm_kernel, L, units, carry, unroll),
        grid=grid,
        in_specs=in_specs,
        out_specs=[col_spec(RET_HEADS), c_spec, n_spec, m_spec],
        out_shape=[jax.ShapeDtypeStruct(mixed.shape, mixed.dtype),
                   jax.ShapeDtypeStruct((batch, h_, d, d), F32),
                   jax.ShapeDtypeStruct((batch, h_, 1, d), F32),
                   jax.ShapeDtypeStruct((batch, h_, 1, GATE_LANES), F32)],
        input_output_aliases={len(args) - 1: 0},
        compiler_params=_cparams(("parallel", "parallel", "arbitrary")),
        name="mlstm_prompt" if carry else "mlstm_sample",
    )(*args)


def _hgrn_kernel(L, units, carry, unroll, *refs):
    if carry:
        (q_ref, f_ref, i_ref, g_ref, lbl_ref, nw_ref, _mixed_in, o_ref, sout_ref, st_ref) = refs
        s0_ref = None
    else:
        (q_ref, f_ref, i_ref, g_ref, lbl_ref, nw_ref, s0_ref, _mixed_in, o_ref, sout_ref) = refs
        st_ref = None

    if carry:
        @pl.when(pl.program_id(2) == 0)
        def _():
            st_ref[...] = jnp.zeros_like(st_ref)

    d = HG_DIM
    lg = lbl_ref[...]
    e = jnp.exp(lg - jnp.max(lg, axis=0, keepdims=True))
    sm = e / jnp.sum(e, axis=0, keepdims=True)
    lb = (sm[0:1, :] + sm[1:2, :]) - sm[0:1, :]
    nw = nw_ref[...]
    groups = L // 8
    pairs = [(j, jp) for jp in range(groups - 1) for j in range(jp + 1, groups)]
    row8 = lax.broadcasted_iota(jnp.int32, (8, d), 0)
    lane = lax.broadcasted_iota(jnp.int32, (8, L), 1)
    if not carry:
        eye_d = (lax.broadcasted_iota(jnp.int32, (d, d), 0)
                 == lax.broadcasted_iota(jnp.int32, (d, d), 1))

    def unit(u, carry_val):
        rows = pl.ds(pl.multiple_of(u * L, L), L)
        q = jax.nn.silu(q_ref[rows, :])
        fg = lb + (1.0 - lb) * jax.nn.sigmoid(f_ref[rows, :])
        k = 1.0 - fg
        logf = jnp.log(fg)
        v = i_ref[rows, :]
        gg = []
        base = jnp.zeros((1, d), F32)
        for j in range(groups):
            x = logf[8 * j:8 * j + 8, :]
            cum = jnp.zeros((8, d), F32)
            for r in range(8):
                cum = cum + jnp.where(row8 >= r, x[r:r + 1, :], 0.0)
            cum = cum + base
            gg.append(cum)
            base = cum[7:8, :]
        g_end = base
        g_cum = jnp.concatenate(gg, axis=0)
        qg = [q[8 * j:8 * j + 8, :] for j in range(groups)]
        kg = [k[8 * j:8 * j + 8, :] for j in range(groups)]
        att = []
        for j in range(groups):
            a = jnp.zeros((8, L), F32)
            for r in range(8):
                diff = jnp.where(row8 >= r, gg[j] - gg[j][r:r + 1, :], -jnp.inf)
                col = jnp.sum(qg[j] * kg[j][r:r + 1, :] * jnp.exp(diff), axis=1, keepdims=True)
                a = jnp.where(lane == 8 * j + r, col, a)
            att.append(a)
        if pairs:
            ends = [gg[j][7:8, :] for j in range(groups)]
            kt = jnp.concatenate([kg[j] * jnp.exp(ends[j] - gg[j]) for j in range(groups)], axis=0)
            qt = jnp.concatenate([qg[j] * jnp.exp(gg[j] - ends[jp]) for j, jp in pairs], axis=0)
            cross = _dot_nt(qt, kt)
            for idx, (j, jp) in enumerate(pairs):
                in_group = (lane >= 8 * jp) & (lane < 8 * jp + 8)
                att[j] = jnp.where(in_group, cross[8 * idx:8 * idx + 8, :], att[j])
        att_m = jnp.concatenate(att, axis=0)
        qs = q * jnp.exp(g_cum)
        kd = k * jnp.exp(g_end - g_cum)
        if carry:
            st = st_ref[...]
            o = _dot_nt(qs, st) + _dot(att_m, v)
            st_ref[...] = st * jnp.exp(g_end) + _dot_tn(v, kd)
        else:
            s = s0_ref[u]
            o = _dot(qs, s) + _dot(att_m, v)
            sout_ref[u] = _row_to_col(jnp.exp(g_end), eye_d) * s + _dot_tn(kd, v)
        out = (_head_norm(o) * nw) * jax.nn.silu(g_ref[rows, :])
        o_ref[rows, :] = out.astype(o_ref.dtype)
        return carry_val

    lax.fori_loop(0, units, unit, 0, unroll=unroll)

    if carry:
        @pl.when(pl.program_id(2) == pl.num_programs(2) - 1)
        def _():
            sout_ref[...] = st_ref[...].T


def _hgrn(proj, lb_logits, norm_w, mixed, *, row0, batch, seq, L, rows_per_step, s0=None, unroll=1):
    d = HG_DIM
    h_ = proj.shape[1] // (4 * d)
    carry = s0 is None
    units = rows_per_step // L
    rb0 = row0 // rows_per_step
    if carry:
        tblocks = seq // rows_per_step
        grid = (batch, h_, tblocks)
        row_idx = lambda b, h, t: rb0 + b * tblocks + t
        sout_spec = pl.BlockSpec((None, None, d, d), lambda b, h, t: (b, h, 0, 0))
        scratch = [pltpu.VMEM((d, d), F32)]
    else:
        nb = rows_per_step // seq
        grid = (batch // nb, h_, 1)
        row_idx = lambda b, h, t: rb0 + b
        sout_spec = pl.BlockSpec((nb, None, d, d), lambda b, h, t: (b, h, 0, 0))
        scratch = []

    def col_spec(c0):
        return pl.BlockSpec((rows_per_step, d), lambda b, h, t: (row_idx(b, h, t), c0 + h))

    in_specs = [col_spec(0), col_spec(h_), col_spec(2 * h_), col_spec(3 * h_),
                pl.BlockSpec((2, d), lambda b, h, t: (0, h)),
                pl.BlockSpec((1, d), lambda b, h, t: (0, 0))]
    args = [proj, proj, proj, proj, lb_logits, norm_w]
    if not carry:
        in_specs.append(sout_spec)
        args.append(s0)
    in_specs.append(pl.BlockSpec(memory_space=pl.ANY))
    args.append(mixed)
    return pl.pallas_call(
        functools.partial(_hgrn_kernel, L, units, carry, unroll),
        grid=grid,
        in_specs=in_specs,
        out_specs=[col_spec(0), sout_spec],
        out_shape=[jax.ShapeDtypeStruct(mixed.shape, mixed.dtype),
                   jax.ShapeDtypeStruct((batch, h_, d, d), F32)],
        scratch_shapes=scratch,
        input_output_aliases={len(args) - 1: 0},
        compiler_params=_cparams(("parallel", "parallel", "arbitrary")),
        name="hgrn_prompt" if carry else "hgrn_sample",
    )(*args)


def _chunk_len(t, c):
    return c if t % c == 0 else t


def _swiglu_residual(x, norm_w, w_gate, w_up, w_down, layer, tm):
    hn = _rmsnorm(x, norm_w, BF16)
    act = _gate_up(hn, w_gate, w_up, layer, tm, 256)
    return _matmul_residual(act, w_down, layer, x, tm, 256, single_buffer_lhs=True,
                            name="down_proj")


def kernel(x_prompt, x_sample, state_ret, state_mlstm_C, state_mlstm_n, state_mlstm_m, state_hgrn,
           norm_mix_w, w_in_ab, b_if_ab, ml_norm_w, w_out_ab, w_in_c, lb_logits, hg_norm_w, w_out_c,
           norm_ffn_w, w_gate, w_up, w_down, norm_final_w):
    bp, tp, d = x_prompt.shape
    bs, ts, _ = x_sample.shape
    mp, ms = bp * tp, bs * ts
    tm = 1024
    x = jnp.concatenate([x_prompt.reshape(mp, d), x_sample.reshape(ms, d)], axis=0)

    w_in_ab_b, w_out_ab_b = w_in_ab.astype(BF16), w_out_ab.astype(BF16)
    w_in_c_b, w_out_c_b = w_in_c.astype(BF16), w_out_c.astype(BF16)
    w_gate_b, w_up_b, w_down_b = w_gate.astype(BF16), w_up.astype(BF16), w_down.astype(BF16)

    ab_main = 4 * RET_HEADS * HEAD_DIM_AB + 4 * ML_HEADS * HEAD_DIM_AB
    hn = _rmsnorm(x, norm_mix_w[0], BF16)
    proj = _matmul(hn, w_in_ab_b, 0, tm, 512, n_out=ab_main, name="in_proj_ab")
    w_gates = jnp.pad(w_in_ab[:1, :, ab_main:],
                      ((0, 0), (0, 0), (0, GATE_LANES - 2 * ML_HEADS))).astype(BF16)
    gates = _matmul(hn, w_gates, 0, tm, GATE_LANES, name="in_proj_gates")
    bias = jnp.pad(b_if_ab[0].astype(F32), (0, GATE_LANES - 2 * ML_HEADS)).reshape(1, GATE_LANES)
    ml_w = ml_norm_w[0].astype(F32).reshape(1, -1)

    mixed = jnp.zeros((mp + ms, d), BF16)
    lp = _chunk_len(tp, RET_CHUNK)
    ls = _chunk_len(ts, RET_CHUNK)
    mixed, ret_p = _retention(proj, mixed, row0=0, batch=bp, seq=tp, L=lp, rows_per_step=4 * lp,
                              pos0=0.0, unroll=UNROLL_AB_PROMPT)
    mixed, ret_s = _retention(proj, mixed, row0=mp, batch=bs, seq=ts, L=ls, rows_per_step=4 * ls,
                              pos0=float(PAST_LEN), s0=state_ret[0], unroll=UNROLL_AB_SAMPLE)
    lp = _chunk_len(tp, ML_CHUNK)
    ls = _chunk_len(ts, ML_CHUNK)
    mixed, mc_p, mn_p, mm_p = _mlstm(proj, gates, bias, ml_w, mixed, row0=0, batch=bp, seq=tp, L=lp,
                                     rows_per_step=4 * lp, unroll=UNROLL_AB_PROMPT)
    m0 = jnp.broadcast_to(state_mlstm_m[0][:, :, None, None], (bs, ML_HEADS, 1, GATE_LANES))
    mixed, mc_s, mn_s, mm_s = _mlstm(proj, gates, bias, ml_w, mixed, row0=mp, batch=bs, seq=ts, L=ls,
                                     rows_per_step=4 * ls, c0=state_mlstm_C[0],
                                     n0=state_mlstm_n[0][:, :, None, :], m0=m0, unroll=UNROLL_AB_SAMPLE)
    x = _matmul_residual(mixed, w_out_ab_b, 0, x, tm, 512, name="out_proj_ab")
    x = _swiglu_residual(x, norm_ffn_w[0], w_gate_b, w_up_b, w_down_b, 0, tm)

    hn = _rmsnorm(x, norm_mix_w[1], BF16)
    proj = _matmul(hn, w_in_c_b, 0, tm, 512, name="in_proj_c")
    mixed = jnp.zeros((mp + ms, d), BF16)
    hg_w = hg_norm_w[0].astype(F32).reshape(1, -1)
    lbl = lb_logits.astype(F32)
    lp = _chunk_len(tp, HG_CHUNK)
    ls = _chunk_len(ts, HG_CHUNK)
    mixed, hg_p = _hgrn(proj, lbl, hg_w, mixed, row0=0, batch=bp, seq=tp, L=lp, rows_per_step=16 * lp,
                        unroll=UNROLL_HG_PROMPT)
    mixed, hg_s = _hgrn(proj, lbl, hg_w, mixed, row0=mp, batch=bs, seq=ts, L=ls, rows_per_step=16 * ls,
                        s0=state_hgrn[0], unroll=UNROLL_HG_SAMPLE)
    x = _matmul_residual(mixed, w_out_c_b, 0, x, tm, 512, name="out_proj_c")
    x = _swiglu_residual(x, norm_ffn_w[1], w_gate_b, w_up_b, w_down_b, 1, tm)

    y_prompt = _rmsnorm(x, norm_final_w, F32, row0=0, rows=mp).reshape(bp, tp, d)
    y_sample = _rmsnorm(x, norm_final_w, F32, row0=mp, rows=ms).reshape(bs, ts, d)
    return (y_prompt, y_sample,
            ret_p[None], mc_p[None], mn_p[:, :, 0, :][None], mm_p[:, :, 0, 0][None], hg_p[None],
            ret_s[None], mc_s[None], mn_s[:, :, 0, :][None], mm_s[:, :, 0, 0][None], hg_s[None])
```

```python
import functools

import numpy as np
import jax
import jax.numpy as jnp
from jax import lax
from jax.experimental import pallas as pl
from jax.experimental.pallas import tpu as pltpu

F32 = jnp.float32
BF16 = jnp.bfloat16

EPS = 1e-6
ROPE_BASE = 10000.0
PAST_LEN = 16384

RET_HEADS = 8
ML_HEADS = 8
HEAD_DIM_AB = 256
HG_DIM = 128
RET_CHUNK = 128
ML_CHUNK = 128
HG_CHUNK = 32
GATE_LANES = 128

V7X_VMEM_LIMIT_BYTES = 56 * 1024 * 1024

UNROLL_AB_PROMPT = 2
UNROLL_AB_SAMPLE = 8
UNROLL_HG_PROMPT = 16
UNROLL_HG_SAMPLE = 8


def _cparams(semantics):
    return pltpu.CompilerParams(dimension_semantics=semantics,
                                vmem_limit_bytes=V7X_VMEM_LIMIT_BYTES)


def _rmsnorm_kernel(x_ref, w_ref, o_ref):
    x = x_ref[...]
    y = x * lax.rsqrt(jnp.mean(x * x, axis=-1, keepdims=True) + EPS)
    o_ref[...] = (y * w_ref[...]).astype(o_ref.dtype)


def _rmsnorm(x, w, out_dtype, tm=512, row0=0, rows=None):
    d = x.shape[1]
    rows = x.shape[0] if rows is None else rows
    rb0 = row0 // tm
    return pl.pallas_call(
        _rmsnorm_kernel,
        grid=(rows // tm,),
        in_specs=[pl.BlockSpec((tm, d), lambda i: (rb0 + i, 0)),
                  pl.BlockSpec((1, d), lambda i: (0, 0))],
        out_specs=pl.BlockSpec((tm, d), lambda i: (i, 0)),
        out_shape=jax.ShapeDtypeStruct((rows, d), out_dtype),
        compiler_params=_cparams(("parallel",)),
        name="rmsnorm",
    )(x, w.reshape(1, d).astype(F32))


def _rmsnorm_stack_kernel(x_ref, w_ref, *rest):
    xo_ref, ho_ref = rest[-2:]
    x = x_ref[...]
    xo_ref[...] = x
    y = x * lax.rsqrt(jnp.mean(x * x, axis=-1, keepdims=True) + EPS)
    ho_ref[...] = (y * w_ref[...]).astype(ho_ref.dtype)


def _rmsnorm_stack(parts, w, tm=512):
    d = parts[0].shape[1]
    m = sum(p.shape[0] for p in parts)
    out_shape = [jax.ShapeDtypeStruct((m, d), F32), jax.ShapeDtypeStruct((m, d), BF16)]
    w2 = w.reshape(1, d).astype(F32)
    outs = None
    row0 = 0
    for part in parts:
        rb0 = row0 // tm
        out_spec = pl.BlockSpec((tm, d), lambda i, rb0=rb0: (rb0 + i, 0))
        in_specs = [pl.BlockSpec((tm, d), lambda i: (i, 0)), pl.BlockSpec((1, d), lambda i: (0, 0))]
        args = [part, w2]
        aliases = {}
        if outs is not None:
            in_specs += [pl.BlockSpec(memory_space=pl.ANY)] * 2
            args += list(outs)
            aliases = {2: 0, 3: 1}
        outs = pl.pallas_call(
            _rmsnorm_stack_kernel,
            grid=(part.shape[0] // tm,),
            in_specs=in_specs,
            out_specs=[out_spec, out_spec],
            out_shape=out_shape,
            input_output_aliases=aliases,
            compiler_params=_cparams(("parallel",)),
            name="rmsnorm_stack",
        )(*args)
        row0 += part.shape[0]
    return outs


def _mm_kernel(x_ref, w_ref, o_ref):
    w = w_ref[...].astype(BF16)
    o_ref[...] = jnp.dot(x_ref[...], w, preferred_element_type=F32).astype(o_ref.dtype)


def _mm_residual_kernel(x_ref, w_ref, r_ref, o_ref):
    w = w_ref[...].astype(BF16)
    o_ref[...] = r_ref[...] + jnp.dot(x_ref[...], w, preferred_element_type=F32)


def _gate_up_kernel(x_ref, wg_ref, wu_ref, o_ref):
    x = x_ref[...]
    a = jnp.dot(x, wg_ref[...].astype(BF16), preferred_element_type=F32)
    b = jnp.dot(x, wu_ref[...].astype(BF16), preferred_element_type=F32)
    o_ref[...] = (jax.nn.silu(a) * b).astype(o_ref.dtype)


def _weight_spec(w, layer, tn, col_block0=0):
    return pl.BlockSpec((None, w.shape[1], tn), lambda i, j: (layer, 0, col_block0 + j))


def _matmul(x, w, layer, tm, tn, n_out=None, col_block0=0, out_dtype=F32, name="matmul"):
    m, k = x.shape
    n = w.shape[2] if n_out is None else n_out
    return pl.pallas_call(
        _mm_kernel,
        grid=(m // tm, n // tn),
        in_specs=[pl.BlockSpec((tm, k), lambda i, j: (i, 0)),
                  _weight_spec(w, layer, tn, col_block0)],
        out_specs=pl.BlockSpec((tm, tn), lambda i, j: (i, j)),
        out_shape=jax.ShapeDtypeStruct((m, n), out_dtype),
        compiler_params=_cparams(("parallel", "arbitrary")),
        name=name,
    )(x, w)


def _matmul_residual(x, w, layer, res, tm, tn, single_buffer_lhs=False, name="matmul_residual"):
    m, k = x.shape
    n = w.shape[2]
    lhs_kwargs = {"pipeline_mode": pl.Buffered(1)} if single_buffer_lhs else {}
    return pl.pallas_call(
        _mm_residual_kernel,
        grid=(m // tm, n // tn),
        in_specs=[pl.BlockSpec((tm, k), lambda i, j: (i, 0), **lhs_kwargs),
                  _weight_spec(w, layer, tn),
                  pl.BlockSpec((tm, tn), lambda i, j: (i, j))],
        out_specs=pl.BlockSpec((tm, tn), lambda i, j: (i, j)),
        out_shape=jax.ShapeDtypeStruct((m, n), F32),
        compiler_params=_cparams(("parallel", "arbitrary")),
        name=name,
    )(x, w, res)


def _gate_up(x, wg, wu, layer, tm, tn):
    m, k = x.shape
    n = wg.shape[2]
    return pl.pallas_call(
        _gate_up_kernel,
        grid=(m // tm, n // tn),
        in_specs=[pl.BlockSpec((tm, k), lambda i, j: (i, 0)),
                  _weight_spec(wg, layer, tn),
                  _weight_spec(wu, layer, tn)],
        out_specs=pl.BlockSpec((tm, tn), lambda i, j: (i, j)),
        out_shape=jax.ShapeDtypeStruct((m, n), BF16),
        compiler_params=_cparams(("parallel", "arbitrary")),
        name="gate_up",
    )(x, wg, wu)


def _dot_nt(a, b):
    return lax.dot_general(a, b, (((1,), (1,)), ((), ())), preferred_element_type=F32)


def _dot_tn(a, b):
    return lax.dot_general(a, b, (((0,), (0,)), ((), ())), preferred_element_type=F32)


def _dot(a, b):
    return jnp.dot(a, b, preferred_element_type=F32)


def _head_norm(x):
    return x * lax.rsqrt(jnp.mean(x * x, axis=-1, keepdims=True) + EPS)


def _col_to_row(col, eye):
    return jnp.sum(jnp.where(eye, col, 0.0), axis=0, keepdims=True)


def _row_to_col(row, eye):
    return jnp.sum(jnp.where(eye, row, 0.0), axis=1, keepdims=True)


def _rotary(x, cos, sin):
    half = x.shape[-1] // 2
    x1, x2 = x[:, :half], x[:, half:]
    return jnp.concatenate([x1 * cos - x2 * sin, x1 * sin + x2 * cos], axis=-1)


def _ret_kernel(L, units, carry, unroll, *refs):
    if carry:
        (q_ref, k_ref, v_ref, g_ref, cos_ref, sin_ref, dec_ref, qd_ref, kd_ref, cd_ref,
         _mixed_in, o_ref, sout_ref) = refs
        s0_ref = None
    else:
        (q_ref, k_ref, v_ref, g_ref, cos_ref, sin_ref, dec_ref, qd_ref, kd_ref, cd_ref,
         s0_ref, _mixed_in, o_ref, sout_ref) = refs

    if carry:
        @pl.when(pl.program_id(2) == 0)
        def _():
            sout_ref[...] = jnp.zeros_like(sout_ref)

    dec = dec_ref[...]
    qd = qd_ref[...]
    kd = kd_ref[...]
    cd = cd_ref[...]
    scale = HEAD_DIM_AB ** -0.5

    def unit(u, c):
        rows = pl.ds(pl.multiple_of(u * L, L), L)
        if carry:
            cos, sin = cos_ref[rows, :], sin_ref[rows, :]
            s = sout_ref[...]
        else:
            cos, sin = cos_ref[...], sin_ref[...]
            s = s0_ref[u]
        q = _rotary(q_ref[rows, :], cos, sin)
        k = _rotary(k_ref[rows, :], cos, sin) * scale
        v = v_ref[rows, :]
        att = _dot_nt(q, k) * dec
        o = _dot(att, v) + _dot(q, s) * qd
        s_new = cd * s + _dot_tn(k * kd, v)
        out = jax.nn.silu(g_ref[rows, :]) * _head_norm(o)
        o_ref[rows, :] = out.astype(o_ref.dtype)
        if carry:
            sout_ref[...] = s_new
        else:
            sout_ref[u] = s_new
        return c

    lax.fori_loop(0, units, unit, 0, unroll=unroll)


def _retention_tables(L):
    log_gamma = jnp.log1p(-jnp.exp2(-5.0 - jnp.arange(RET_HEADS, dtype=F32)))[:, None]
    idx = jnp.arange(L, dtype=F32)
    diff = idx[:, None] - idx[None, :]
    decay = jnp.exp(jnp.where(diff >= 0, log_gamma[:, :, None] * diff, -jnp.inf))
    q_decay = jnp.exp(log_gamma * (idx + 1.0))[:, :, None]
    k_decay = jnp.exp(log_gamma * (L - 1.0 - idx))[:, :, None]
    chunk_decay = jnp.exp(log_gamma[:, 0] * L)[:, None, None]
    return decay, q_decay, k_decay, chunk_decay


def _rope_tables(pos):
    half = HEAD_DIM_AB // 2
    inv_freq = ROPE_BASE ** (-jnp.arange(half, dtype=F32) / half)
    ang = pos[:, None] * inv_freq[None, :]
    return jnp.cos(ang), jnp.sin(ang)


def _retention(proj, mixed, *, row0, batch, seq, L, rows_per_step, pos0, s0=None, unroll=1):
    d = HEAD_DIM_AB
    h_ = RET_HEADS
    carry = s0 is None
    units = rows_per_step // L
    decay, q_decay, k_decay, chunk_decay = _retention_tables(L)
    cos, sin = _rope_tables(pos0 + jnp.arange(seq, dtype=F32))
    rb0 = row0 // rows_per_step
    if carry:
        tblocks = seq // rows_per_step
        grid = (batch, h_, tblocks)
        row_idx = lambda b, h, t: rb0 + b * tblocks + t
        rope_spec = pl.BlockSpec((rows_per_step, d // 2), lambda b, h, t: (t, 0))
        sout_spec = pl.BlockSpec((None, None, d, d), lambda b, h, t: (b, h, 0, 0))
    else:
        nb = rows_per_step // seq
        grid = (batch // nb, h_, 1)
        row_idx = lambda b, h, t: rb0 + b
        rope_spec = pl.BlockSpec((seq, d // 2), lambda b, h, t: (0, 0))
        sout_spec = pl.BlockSpec((nb, None, d, d), lambda b, h, t: (b, h, 0, 0))

    def col_spec(c0):
        return pl.BlockSpec((rows_per_step, d), lambda b, h, t: (row_idx(b, h, t), c0 + h))

    in_specs = [col_spec(0), col_spec(h_), col_spec(2 * h_), col_spec(3 * h_),
                rope_spec, rope_spec,
                pl.BlockSpec((None, L, L), lambda b, h, t: (h, 0, 0)),
                pl.BlockSpec((None, L, 1), lambda b, h, t: (h, 0, 0)),
                pl.BlockSpec((None, L, 1), lambda b, h, t: (h, 0, 0)),
                pl.BlockSpec((None, 1, 1), lambda b, h, t: (h, 0, 0))]
    args = [proj, proj, proj, proj, cos, sin, decay, q_decay, k_decay, chunk_decay]
    if not carry:
        in_specs.append(sout_spec)
        args.append(s0)
    in_specs.append(pl.BlockSpec(memory_space=pl.ANY))
    args.append(mixed)
    return pl.pallas_call(
        functools.partial(_ret_kernel, L, units, carry, unroll),
        grid=grid,
        in_specs=in_specs,
        out_specs=[col_spec(0), sout_spec],
        out_shape=[jax.ShapeDtypeStruct(mixed.shape, mixed.dtype),
                   jax.ShapeDtypeStruct((batch, h_, d, d), F32)],
        input_output_aliases={len(args) - 1: 0},
        compiler_params=_cparams(("parallel", "parallel", "arbitrary")),
        name="retention_prompt" if carry else "retention_sample",
    )(*args)


def _mlstm_kernel(L, units, carry, unroll, *refs):
    if carry:
        (q_ref, k_ref, v_ref, og_ref, gates_ref, bias_ref, nw_ref,
         _mixed_in, o_ref, c_ref, n_ref, m_ref) = refs
        c0_ref = n0_ref = m0_ref = None
    else:
        (q_ref, k_ref, v_ref, og_ref, gates_ref, bias_ref, nw_ref, c0_ref, n0_ref, m0_ref,
         _mixed_in, o_ref, c_ref, n_ref, m_ref) = refs

    if carry:
        @pl.when(pl.program_id(2) == 0)
        def _():
            c_ref[...] = jnp.zeros_like(c_ref)
            n_ref[...] = jnp.zeros_like(n_ref)
            m_ref[...] = jnp.zeros_like(m_ref)

    head = pl.program_id(1)
    bias = bias_ref[...]
    nw = nw_ref[...]
    lane = lax.broadcasted_iota(jnp.int32, (L, GATE_LANES), 1)
    ti = lax.broadcasted_iota(jnp.int32, (L, L), 0)
    si = lax.broadcasted_iota(jnp.int32, (L, L), 1)
    eye = ti == si
    causal = si <= ti
    scale = HEAD_DIM_AB ** -0.5

    def unit(u, carry_val):
        rows = pl.ds(pl.multiple_of(u * L, L), L)
        if carry:
            c, n, m = c_ref[...], n_ref[...], m_ref[:, 0:1]
        else:
            c, n, m = c0_ref[u], n0_ref[u], m0_ref[u][:, 0:1]
        x = gates_ref[rows, :] + bias
        ig = jnp.sum(jnp.where(lane == head, x, 0.0), axis=1, keepdims=True)
        fpre = jnp.sum(jnp.where(lane == head + ML_HEADS, x, 0.0), axis=1, keepdims=True)
        lf = jnp.minimum(fpre, 0.0) - jnp.log1p(jnp.exp(-jnp.abs(fpre)))
        f_cum = jnp.sum(jnp.where(causal, _col_to_row(lf, eye), 0.0), axis=1, keepdims=True)
        a = ig - f_cum
        a_row = _col_to_row(a, eye)
        cmax = jnp.max(jnp.where(causal, a_row, -jnp.inf), axis=1, keepdims=True)
        m_t = f_cum + jnp.maximum(m, cmax)
        d_mat = jnp.exp(jnp.where(causal, (f_cum - m_t) + a_row, -jnp.inf))
        q = q_ref[rows, :]
        k = k_ref[rows, :] * scale
        v = v_ref[rows, :]
        s = _dot_nt(q, k) * d_mat
        inter = jnp.exp(f_cum + m - m_t)
        num = inter * _dot(q, c) + _dot(s, v)
        den = inter * jnp.sum(q * n, axis=1, keepdims=True) + jnp.sum(s, axis=1, keepdims=True)
        hh = num * (1.0 / jnp.maximum(jnp.abs(den), jnp.exp(-m_t)))
        f_end = f_cum[L - 1:L, :]
        m_end = m_t[L - 1:L, :]
        w = jnp.exp(f_end - m_end + a)
        cdec = jnp.exp(f_end + m - m_end)
        kw = k * w
        c_new = cdec * c + _dot_tn(kw, v)
        n_new = cdec * n + jnp.sum(kw, axis=0, keepdims=True)
        out = jax.nn.sigmoid(og_ref[rows, :]) * (_head_norm(hh) * nw)
        o_ref[rows, :] = out.astype(o_ref.dtype)
        m_new = jnp.broadcast_to(m_end, (1, GATE_LANES))
        if carry:
            c_ref[...] = c_new
            n_ref[...] = n_new
            m_ref[...] = m_new
        else:
            c_ref[u] = c_new
            n_ref[u] = n_new
            m_ref[u] = m_new
        return carry_val

    lax.fori_loop(0, units, unit, 0, unroll=unroll)


def _mlstm(proj, gates, bias, norm_w, mixed, *, row0, batch, seq, L, rows_per_step,
           c0=None, n0=None, m0=None, unroll=1):
    d = HEAD_DIM_AB
    h_ = ML_HEADS
    carry = c0 is None
    units = rows_per_step // L
    rb0 = row0 // rows_per_step
    col0 = 4 * RET_HEADS
    if carry:
        tblocks = seq // rows_per_step
        grid = (batch, h_, tblocks)
        row_idx = lambda b, h, t: rb0 + b * tblocks + t
        lead = None
    else:
        nb = rows_per_step // seq
        grid = (batch // nb, h_, 1)
        row_idx = lambda b, h, t: rb0 + b
        lead = nb
    c_spec = pl.BlockSpec((lead, None, d, d), lambda b, h, t: (b, h, 0, 0))
    n_spec = pl.BlockSpec((lead, None, 1, d), lambda b, h, t: (b, h, 0, 0))
    m_spec = pl.BlockSpec((lead, None, 1, GATE_LANES), lambda b, h, t: (b, h, 0, 0))

    def col_spec(c_blk):
        return pl.BlockSpec((rows_per_step, d), lambda b, h, t: (row_idx(b, h, t), c_blk + h))

    in_specs = [col_spec(col0), col_spec(col0 + h_), col_spec(col0 + 2 * h_), col_spec(col0 + 3 * h_),
                pl.BlockSpec((rows_per_step, GATE_LANES), lambda b, h, t: (row_idx(b, h, t), 0)),
                pl.BlockSpec((1, GATE_LANES), lambda b, h, t: (0, 0)),
                pl.BlockSpec((1, d), lambda b, h, t: (0, h))]
    args = [proj, proj, proj, proj, gates, bias, norm_w]
    if not carry:
        in_specs += [c_spec, n_spec, m_spec]
        args += [c0, n0, m0]
    in_specs.append(pl.BlockSpec(memory_space=pl.ANY))
    args.append(mixed)
    return pl.pallas_call(
        functools.partial(_mlstm_kernel, L, units, carry, unroll),
        grid=grid,
        in_specs=in_specs,
        out_specs=[col_spec(RET_HEADS), c_spec, n_spec, m_spec],
        out_shape=[jax.ShapeDtypeStruct(mixed.shape, mixed.dtype),
                   jax.ShapeDtypeStruct((batch, h_, d, d), F32),
                   jax.ShapeDtypeStruct((batch, h_, 1, d), F32),
                   jax.ShapeDtypeStruct((batch, h_, 1, GATE_LANES), F32)],
        input_output_aliases={len(args) - 1: 0},
        compiler_params=_cparams(("parallel", "parallel", "arbitrary")),
        name="mlstm_prompt" if carry else "mlstm_sample",
    )(*args)


def _hgrn_kernel(L, units, carry, unroll, *refs):
    if carry:
        (q_ref, f_ref, i_ref, g_ref, lbl_ref, nw_ref, _mixed_in, o_ref, sout_ref, st_ref) = refs
        s0_ref = None
    else:
        (q_ref, f_ref, i_ref, g_ref, lbl_ref, nw_ref, s0_ref, _mixed_in, o_ref, sout_ref) = refs
        st_ref = None

    if carry:
        @pl.when(pl.program_id(2) == 0)
        def _():
            st_ref[...] = jnp.zeros_like(st_ref)

    d = HG_DIM
    lg = lbl_ref[...]
    e = jnp.exp(lg - jnp.max(lg, axis=0, keepdims=True))
    sm = e / jnp.sum(e, axis=0, keepdims=True)
    lb = (sm[0:1, :] + sm[1:2, :]) - sm[0:1, :]
    nw = nw_ref[...]
    groups = L // 8
    pairs = [(j, jp) for jp in range(groups - 1) for j in range(jp + 1, groups)]
    row8 = lax.broadcasted_iota(jnp.int32, (8, d), 0)
    lane = lax.broadcasted_iota(jnp.int32, (8, L), 1)
    if not carry:
        eye_d = (lax.broadcasted_iota(jnp.int32, (d, d), 0)
                 == lax.broadcasted_iota(jnp.int32, (d, d), 1))

    def unit(u, carry_val):
        rows = pl.ds(pl.multiple_of(u * L, L), L)
        q = jax.nn.silu(q_ref[rows, :])
        fg = lb + (1.0 - lb) * jax.nn.sigmoid(f_ref[rows, :])
        k = 1.0 - fg
        logf = jnp.log(fg)
        v = i_ref[rows, :]
        gg = []
        base = jnp.zeros((1, d), F32)
        for j in range(groups):
            x = logf[8 * j:8 * j + 8, :]
            cum = jnp.zeros((8, d), F32)
            for r in range(8):
                cum = cum + jnp.where(row8 >= r, x[r:r + 1, :], 0.0)
            cum = cum + base
            gg.append(cum)
            base = cum[7:8, :]
        g_end = base
        g_cum = jnp.concatenate(gg, axis=0)
        qg = [q[8 * j:8 * j + 8, :] for j in range(groups)]
        kg = [k[8 * j:8 * j + 8, :] for j in range(groups)]
        att = []
        for j in range(groups):
            a = jnp.zeros((8, L), F32)
            for r in range(8):
                diff = jnp.where(row8 >= r, gg[j] - gg[j][r:r + 1, :], -jnp.inf)
                col = jnp.sum(qg[j] * kg[j][r:r + 1, :] * jnp.exp(diff), axis=1, keepdims=True)
                a = jnp.where(lane == 8 * j + r, col, a)
            att.append(a)
        if pairs:
            ends = [gg[j][7:8, :] for j in range(groups)]
            kt = jnp.concatenate([kg[j] * jnp.exp(ends[j] - gg[j]) for j in range(groups)], axis=0)
            qt = jnp.concatenate([qg[j] * jnp.exp(gg[j] - ends[jp]) for j, jp in pairs], axis=0)
            cross = _dot_nt(qt, kt)
            for idx, (j, jp) in enumerate(pairs):
                in_group = (lane >= 8 * jp) & (lane < 8 * jp + 8)
                att[j] = jnp.where(in_group, cross[8 * idx:8 * idx + 8, :], att[j])
        att_m = jnp.concatenate(att, axis=0)
        qs = q * jnp.exp(g_cum)
        kd = k * jnp.exp(g_end - g_cum)
        if carry:
            st = st_ref[...]
            o = _dot_nt(qs, st) + _dot(att_m, v)
            st_ref[...] = st * jnp.exp(g_end) + _dot_tn(v, kd)
        else:
            s = s0_ref[u]
            o = _dot(qs, s) + _dot(att_m, v)
            sout_ref[u] = _row_to_col(jnp.exp(g_end), eye_d) * s + _dot_tn(kd, v)
        out = (_head_norm(o) * nw) * jax.nn.silu(g_ref[rows, :])
        o_ref[rows, :] = out.astype(o_ref.dtype)
        return carry_val

    lax.fori_loop(0, units, unit, 0, unroll=unroll)

    if carry:
        @pl.when(pl.program_id(2) == pl.num_programs(2) - 1)
        def _():
            sout_ref[...] = st_ref[...].T


def _hgrn(proj, lb_logits, norm_w, mixed, *, row0, batch, seq, L, rows_per_step, s0=None, unroll=1):
    d = HG_DIM
    h_ = proj.shape[1] // (4 * d)
    carry = s0 is None
    units = rows_per_step // L
    rb0 = row0 // rows_per_step
    if carry:
        tblocks = seq // rows_per_step
        grid = (batch, h_, tblocks)
        row_idx = lambda b, h, t: rb0 + b * tblocks + t
        sout_spec = pl.BlockSpec((None, None, d, d), lambda b, h, t: (b, h, 0, 0))
        scratch = [pltpu.VMEM((d, d), F32)]
    else:
        nb = rows_per_step // seq
        grid = (batch // nb, h_, 1)
        row_idx = lambda b, h, t: rb0 + b
        sout_spec = pl.BlockSpec((nb, None, d, d), lambda b, h, t: (b, h, 0, 0))
        scratch = []

    def col_spec(c0):
        return pl.BlockSpec((rows_per_step, d), lambda b, h, t: (row_idx(b, h, t), c0 + h))

    in_specs = [col_spec(0), col_spec(h_), col_spec(2 * h_), col_spec(3 * h_),
                pl.BlockSpec((2, d), lambda b, h, t: (0, h)),
                pl.BlockSpec((1, d), lambda b, h, t: (0, 0))]
    args = [proj, proj, proj, proj, lb_logits, norm_w]
    if not carry:
        in_specs.append(sout_spec)
        args.append(s0)
    in_specs.append(pl.BlockSpec(memory_space=pl.ANY))
    args.append(mixed)
    return pl.pallas_call(
        functools.partial(_hgrn_kernel, L, units, carry, unroll),
        grid=grid,
        in_specs=in_specs,
        out_specs=[col_spec(0), sout_spec],
        out_shape=[jax.ShapeDtypeStruct(mixed.shape, mixed.dtype),
                   jax.ShapeDtypeStruct((batch, h_, d, d), F32)],
        scratch_shapes=scratch,
        input_output_aliases={len(args) - 1: 0},
        compiler_params=_cparams(("parallel", "parallel", "arbitrary")),
        name="hgrn_prompt" if carry else "hgrn_sample",
    )(*args)


def _chunk_len(t, c):
    return c if t % c == 0 else t


def _swiglu_residual(x, norm_w, w_gate, w_up, w_down, layer, tm):
    hn = _rmsnorm(x, norm_w, BF16)
    act = _gate_up(hn, w_gate, w_up, layer, tm, 256)
    return _matmul_residual(act, w_down, layer, x, tm, 256, single_buffer_lhs=True,
                            name="down_proj")


def kernel(x_prompt, x_sample, state_ret, state_mlstm_C, state_mlstm_n, state_mlstm_m, state_hgrn,
           norm_mix_w, w_in_ab, b_if_ab, ml_norm_w, w_out_ab, w_in_c, lb_logits, hg_norm_w, w_out_c,
           norm_ffn_w, w_gate, w_up, w_down, norm_final_w):
    bp, tp, d = x_prompt.shape
    bs, ts, _ = x_sample.shape
    mp, ms = bp * tp, bs * ts
    tm = 1024

    w_in_ab_b, w_out_ab_b = w_in_ab, w_out_ab
    w_in_c_b, w_out_c_b = w_in_c, w_out_c
    w_gate_b, w_up_b, w_down_b = w_gate, w_up, w_down

    ab_main = 4 * RET_HEADS * HEAD_DIM_AB + 4 * ML_HEADS * HEAD_DIM_AB
    x, hn = _rmsnorm_stack([x_prompt.reshape(mp, d), x_sample.reshape(ms, d)], norm_mix_w[0])
    proj = _matmul(hn, w_in_ab_b, 0, tm, 512, n_out=ab_main, name="in_proj_ab")
    gates = _matmul(hn, w_in_ab_b, 0, tm, GATE_LANES, n_out=GATE_LANES,
                    col_block0=ab_main // GATE_LANES, name="in_proj_gates")
    bias = jnp.pad(b_if_ab[0].astype(F32), (0, GATE_LANES - 2 * ML_HEADS)).reshape(1, GATE_LANES)
    ml_w = ml_norm_w[0].astype(F32).reshape(1, -1)

    mixed = jnp.zeros((mp + ms, d), BF16)
    lp = _chunk_len(tp, RET_CHUNK)
    ls = _chunk_len(ts, RET_CHUNK)
    mixed, ret_p = _retention(proj, mixed, row0=0, batch=bp, seq=tp, L=lp, rows_per_step=4 * lp,
                              pos0=0.0, unroll=UNROLL_AB_PROMPT)
    mixed, ret_s = _retention(proj, mixed, row0=mp, batch=bs, seq=ts, L=ls, rows_per_step=8 * ls,
                              pos0=float(PAST_LEN), s0=state_ret[0], unroll=UNROLL_AB_SAMPLE)
    lp = _chunk_len(tp, ML_CHUNK)
    ls = _chunk_len(ts, ML_CHUNK)
    mixed, mc_p, mn_p, mm_p = _mlstm(proj, gates, bias, ml_w, mixed, row0=0, batch=bp, seq=tp, L=lp,
                                     rows_per_step=4 * lp, unroll=UNROLL_AB_PROMPT)
    m0 = jnp.broadcast_to(state_mlstm_m[0][:, :, None, None], (bs, ML_HEADS, 1, GATE_LANES))
    mixed, mc_s, mn_s, mm_s = _mlstm(proj, gates, bias, ml_w, mixed, row0=mp, batch=bs, seq=ts, L=ls,
                                     rows_per_step=8 * ls, c0=state_mlstm_C[0],
                                     n0=state_mlstm_n[0][:, :, None, :], m0=m0, unroll=UNROLL_AB_SAMPLE)
    x = _matmul_residual(mixed, w_out_ab_b, 0, x, tm, 512, name="out_proj_ab")
    x = _swiglu_residual(x, norm_ffn_w[0], w_gate_b, w_up_b, w_down_b, 0, tm)

    hn = _rmsnorm(x, norm_mix_w[1], BF16)
    proj = _matmul(hn, w_in_c_b, 0, tm, 512, name="in_proj_c")
    mixed = jnp.zeros((mp + ms, d), BF16)
    hg_w = hg_norm_w[0].astype(F32).reshape(1, -1)
    lbl = lb_logits.astype(F32)
    lp = _chunk_len(tp, HG_CHUNK)
    ls = _chunk_len(ts, HG_CHUNK)
    mixed, hg_p = _hgrn(proj, lbl, hg_w, mixed, row0=0, batch=bp, seq=tp, L=lp, rows_per_step=16 * lp,
                        unroll=UNROLL_HG_PROMPT)
    mixed, hg_s = _hgrn(proj, lbl, hg_w, mixed, row0=mp, batch=bs, seq=ts, L=ls, rows_per_step=16 * ls,
                        s0=state_hgrn[0], unroll=UNROLL_HG_SAMPLE)
    x = _matmul_residual(mixed, w_out_c_b, 0, x, tm, 512, name="out_proj_c")
    x = _swiglu_residual(x, norm_ffn_w[1], w_gate_b, w_up_b, w_down_b, 1, tm)

    y_prompt = _rmsnorm(x, norm_final_w, F32, row0=0, rows=mp).reshape(bp, tp, d)
    y_sample = _rmsnorm(x, norm_final_w, F32, row0=mp, rows=ms).reshape(bs, ts, d)
    return (y_prompt, y_sample,
            ret_p[None], mc_p[None], mn_p[:, :, 0, :][None], mm_p[:, :, 0, 0][None], hg_p[None],
            ret_s[None], mc_s[None], mn_s[:, :, 0, :][None], mm_s[:, :, 0, 0][None], hg_s[None])
```

```python
import functools

import numpy as np
import jax
import jax.numpy as jnp
from jax import lax
from jax.experimental import pallas as pl
from jax.experimental.pallas import tpu as pltpu

F32 = jnp.float32
BF16 = jnp.bfloat16

EPS = 1e-6
ROPE_BASE = 10000.0
PAST_LEN = 16384

RET_HEADS = 8
ML_HEADS = 8
HEAD_DIM_AB = 256
HG_DIM = 128
RET_CHUNK = 128
ML_CHUNK = 128
HG_CHUNK = 32
GATE_LANES = 128

V7X_VMEM_LIMIT_BYTES = 56 * 1024 * 1024

TM_WIDE = 1536
TM_DOWN = 1024

UNROLL_AB_PROMPT = 2
UNROLL_AB_SAMPLE = 8
UNROLL_HG_PROMPT = 16
UNROLL_HG_SAMPLE = 8


def _cparams(semantics):
    return pltpu.CompilerParams(dimension_semantics=semantics,
                                vmem_limit_bytes=V7X_VMEM_LIMIT_BYTES)


def _rmsnorm_kernel(x_ref, w_ref, o_ref):
    x = x_ref[...]
    y = x * lax.rsqrt(jnp.mean(x * x, axis=-1, keepdims=True) + EPS)
    o_ref[...] = (y * w_ref[...]).astype(o_ref.dtype)


def _rmsnorm(x, w, out_dtype, tm=512, row0=0, rows=None):
    d = x.shape[1]
    rows = x.shape[0] if rows is None else rows
    rb0 = row0 // tm
    return pl.pallas_call(
        _rmsnorm_kernel,
        grid=(rows // tm,),
        in_specs=[pl.BlockSpec((tm, d), lambda i: (rb0 + i, 0)),
                  pl.BlockSpec((1, d), lambda i: (0, 0))],
        out_specs=pl.BlockSpec((tm, d), lambda i: (i, 0)),
        out_shape=jax.ShapeDtypeStruct((rows, d), out_dtype),
        compiler_params=_cparams(("parallel",)),
        name="rmsnorm",
    )(x, w.reshape(1, d).astype(F32))


def _rmsnorm_stack_kernel(x_ref, w_ref, *rest):
    xo_ref, ho_ref = rest[-2:]
    x = x_ref[...]
    xo_ref[...] = x
    y = x * lax.rsqrt(jnp.mean(x * x, axis=-1, keepdims=True) + EPS)
    ho_ref[...] = (y * w_ref[...]).astype(ho_ref.dtype)


def _rmsnorm_stack(parts, w, tm=512):
    d = parts[0].shape[1]
    m = sum(p.shape[0] for p in parts)
    out_shape = [jax.ShapeDtypeStruct((m, d), F32), jax.ShapeDtypeStruct((m, d), BF16)]
    w2 = w.reshape(1, d).astype(F32)
    outs = None
    row0 = 0
    for part in parts:
        rb0 = row0 // tm
        out_spec = pl.BlockSpec((tm, d), lambda i, rb0=rb0: (rb0 + i, 0))
        in_specs = [pl.BlockSpec((tm, d), lambda i: (i, 0)), pl.BlockSpec((1, d), lambda i: (0, 0))]
        args = [part, w2]
        aliases = {}
        if outs is not None:
            in_specs += [pl.BlockSpec(memory_space=pl.ANY)] * 2
            args += list(outs)
            aliases = {2: 0, 3: 1}
        outs = pl.pallas_call(
            _rmsnorm_stack_kernel,
            grid=(part.shape[0] // tm,),
            in_specs=in_specs,
            out_specs=[out_spec, out_spec],
            out_shape=out_shape,
            input_output_aliases=aliases,
            compiler_params=_cparams(("parallel",)),
            name="rmsnorm_stack",
        )(*args)
        row0 += part.shape[0]
    return outs


def _mm_kernel(x_ref, w_ref, o_ref):
    w = w_ref[...].astype(BF16)
    o_ref[...] = jnp.dot(x_ref[...], w, preferred_element_type=F32).astype(o_ref.dtype)


def _mm_residual_kernel(x_ref, w_ref, r_ref, o_ref):
    w = w_ref[...].astype(BF16)
    o_ref[...] = r_ref[...] + jnp.dot(x_ref[...], w, preferred_element_type=F32)


def _gate_up_kernel(x_ref, wg_ref, wu_ref, o_ref):
    x = x_ref[...]
    a = jnp.dot(x, wg_ref[...].astype(BF16), preferred_element_type=F32)
    b = jnp.dot(x, wu_ref[...].astype(BF16), preferred_element_type=F32)
    o_ref[...] = (jax.nn.silu(a) * b).astype(o_ref.dtype)


def _weight_spec(w, layer, tn, col_block0=0):
    return pl.BlockSpec((None, w.shape[1], tn), lambda i, j: (layer, 0, col_block0 + j))


def _matmul(x, w, layer, tm, tn, n_out=None, col_block0=0, out_dtype=F32, name="matmul"):
    m, k = x.shape
    n = w.shape[2] if n_out is None else n_out
    return pl.pallas_call(
        _mm_kernel,
        grid=(m // tm, n // tn),
        in_specs=[pl.BlockSpec((tm, k), lambda i, j: (i, 0)),
                  _weight_spec(w, layer, tn, col_block0)],
        out_specs=pl.BlockSpec((tm, tn), lambda i, j: (i, j)),
        out_shape=jax.ShapeDtypeStruct((m, n), out_dtype),
        compiler_params=_cparams(("parallel", "arbitrary")),
        name=name,
    )(x, w)


def _matmul_residual(x, w, layer, res, tm, tn, single_buffer_lhs=False, name="matmul_residual"):
    m, k = x.shape
    n = w.shape[2]
    lhs_kwargs = {"pipeline_mode": pl.Buffered(1)} if single_buffer_lhs else {}
    return pl.pallas_call(
        _mm_residual_kernel,
        grid=(m // tm, n // tn),
        in_specs=[pl.BlockSpec((tm, k), lambda i, j: (i, 0), **lhs_kwargs),
                  _weight_spec(w, layer, tn),
                  pl.BlockSpec((tm, tn), lambda i, j: (i, j))],
        out_specs=pl.BlockSpec((tm, tn), lambda i, j: (i, j)),
        out_shape=jax.ShapeDtypeStruct((m, n), F32),
        compiler_params=_cparams(("parallel", "arbitrary")),
        name=name,
    )(x, w, res)


def _gate_up(x, wg, wu, layer, tm, tn):
    m, k = x.shape
    n = wg.shape[2]
    return pl.pallas_call(
        _gate_up_kernel,
        grid=(m // tm, n // tn),
        in_specs=[pl.BlockSpec((tm, k), lambda i, j: (i, 0)),
                  _weight_spec(wg, layer, tn),
                  _weight_spec(wu, layer, tn)],
        out_specs=pl.BlockSpec((tm, tn), lambda i, j: (i, j)),
        out_shape=jax.ShapeDtypeStruct((m, n), BF16),
        compiler_params=_cparams(("parallel", "arbitrary")),
        name="gate_up",
    )(x, wg, wu)


def _dot_nt(a, b):
    return lax.dot_general(a, b, (((1,), (1,)), ((), ())), preferred_element_type=F32)


def _dot_tn(a, b):
    return lax.dot_general(a, b, (((0,), (0,)), ((), ())), preferred_element_type=F32)


def _dot(a, b):
    return jnp.dot(a, b, preferred_element_type=F32)


def _head_norm(x):
    return x * lax.rsqrt(jnp.mean(x * x, axis=-1, keepdims=True) + EPS)


def _col_to_row(col, eye):
    return jnp.sum(jnp.where(eye, col, 0.0), axis=0, keepdims=True)


def _row_to_col(row, eye):
    return jnp.sum(jnp.where(eye, row, 0.0), axis=1, keepdims=True)


def _alias_existing(mixed, in_specs, args):
    if isinstance(mixed, jax.ShapeDtypeStruct):
        return {}
    in_specs.append(pl.BlockSpec(memory_space=pl.ANY))
    args.append(mixed)
    return {len(args) - 1: 0}


def _rotary(x, cos, sin):
    half = x.shape[-1] // 2
    x1, x2 = x[:, :half], x[:, half:]
    return jnp.concatenate([x1 * cos - x2 * sin, x1 * sin + x2 * cos], axis=-1)


def _ret_kernel(L, units, carry, unroll, *refs):
    (q_ref, k_ref, v_ref, g_ref, cos_ref, sin_ref, dec_ref, qd_ref, kd_ref, cd_ref) = refs[:10]
    s0_ref = None if carry else refs[10]
    o_ref, sout_ref = refs[-2:]

    if carry:
        @pl.when(pl.program_id(2) == 0)
        def _():
            sout_ref[...] = jnp.zeros_like(sout_ref)

    dec = dec_ref[...]
    qd = qd_ref[...]
    kd = kd_ref[...]
    cd = cd_ref[...]
    scale = HEAD_DIM_AB ** -0.5

    def unit(u, c):
        rows = pl.ds(pl.multiple_of(u * L, L), L)
        if carry:
            cos, sin = cos_ref[rows, :], sin_ref[rows, :]
            s = sout_ref[...]
        else:
            cos, sin = cos_ref[...], sin_ref[...]
            s = s0_ref[u]
        q = _rotary(q_ref[rows, :], cos, sin)
        k = _rotary(k_ref[rows, :], cos, sin) * scale
        v = v_ref[rows, :]
        att = _dot_nt(q, k) * dec
        o = _dot(att, v) + _dot(q, s) * qd
        s_new = cd * s + _dot_tn(k * kd, v)
        out = jax.nn.silu(g_ref[rows, :]) * _head_norm(o)
        o_ref[rows, :] = out.astype(o_ref.dtype)
        if carry:
            sout_ref[...] = s_new
        else:
            sout_ref[u] = s_new
        return c

    lax.fori_loop(0, units, unit, 0, unroll=unroll)


def _retention_tables(L):
    log_gamma = jnp.log1p(-jnp.exp2(-5.0 - jnp.arange(RET_HEADS, dtype=F32)))[:, None]
    idx = jnp.arange(L, dtype=F32)
    diff = idx[:, None] - idx[None, :]
    decay = jnp.exp(jnp.where(diff >= 0, log_gamma[:, :, None] * diff, -jnp.inf))
    q_decay = jnp.exp(log_gamma * (idx + 1.0))[:, :, None]
    k_decay = jnp.exp(log_gamma * (L - 1.0 - idx))[:, :, None]
    chunk_decay = jnp.exp(log_gamma[:, 0] * L)[:, None, None]
    return decay, q_decay, k_decay, chunk_decay


def _rope_tables(pos):
    half = HEAD_DIM_AB // 2
    inv_freq = ROPE_BASE ** (-jnp.arange(half, dtype=F32) / half)
    ang = pos[:, None] * inv_freq[None, :]
    return jnp.cos(ang), jnp.sin(ang)


def _retention(proj, mixed, *, row0, batch, seq, L, rows_per_step, pos0, s0=None, unroll=1):
    d = HEAD_DIM_AB
    h_ = RET_HEADS
    carry = s0 is None
    units = rows_per_step // L
    decay, q_decay, k_decay, chunk_decay = _retention_tables(L)
    cos, sin = _rope_tables(pos0 + jnp.arange(seq, dtype=F32))
    rb0 = row0 // rows_per_step
    if carry:
        tblocks = seq // rows_per_step
        grid = (batch, h_, tblocks)
        row_idx = lambda b, h, t: rb0 + b * tblocks + t
        rope_spec = pl.BlockSpec((rows_per_step, d // 2), lambda b, h, t: (t, 0))
        sout_spec = pl.BlockSpec((None, None, d, d), lambda b, h, t: (b, h, 0, 0))
    else:
        nb = rows_per_step // seq
        grid = (batch // nb, h_, 1)
        row_idx = lambda b, h, t: rb0 + b
        rope_spec = pl.BlockSpec((seq, d // 2), lambda b, h, t: (0, 0))
        sout_spec = pl.BlockSpec((nb, None, d, d), lambda b, h, t: (b, h, 0, 0))

    def col_spec(c0):
        return pl.BlockSpec((rows_per_step, d), lambda b, h, t: (row_idx(b, h, t), c0 + h))

    in_specs = [col_spec(0), col_spec(h_), col_spec(2 * h_), col_spec(3 * h_),
                rope_spec, rope_spec,
                pl.BlockSpec((None, L, L), lambda b, h, t: (h, 0, 0)),
                pl.BlockSpec((None, L, 1), lambda b, h, t: (h, 0, 0)),
                pl.BlockSpec((None, L, 1), lambda b, h, t: (h, 0, 0)),
                pl.BlockSpec((None, 1, 1), lambda b, h, t: (h, 0, 0))]
    args = [proj, proj, proj, proj, cos, sin, decay, q_decay, k_decay, chunk_decay]
    if not carry:
        in_specs.append(sout_spec)
        args.append(s0)
    aliases = _alias_existing(mixed, in_specs, args)
    return pl.pallas_call(
        functools.partial(_ret_kernel, L, units, carry, unroll),
        grid=grid,
        in_specs=in_specs,
        out_specs=[col_spec(0), sout_spec],
        out_shape=[jax.ShapeDtypeStruct(mixed.shape, mixed.dtype),
                   jax.ShapeDtypeStruct((batch, h_, d, d), F32)],
        input_output_aliases=aliases,
        compiler_params=_cparams(("parallel", "parallel", "arbitrary")),
        name="retention_prompt" if carry else "retention_sample",
    )(*args)


def _mlstm_kernel(L, units, carry, unroll, *refs):
    (q_ref, k_ref, v_ref, og_ref, gates_ref, bias_ref, nw_ref) = refs[:7]
    c0_ref, n0_ref, m0_ref = (None, None, None) if carry else refs[7:10]
    o_ref, c_ref, n_ref, m_ref = refs[-4:]

    if carry:
        @pl.when(pl.program_id(2) == 0)
        def _():
            c_ref[...] = jnp.zeros_like(c_ref)
            n_ref[...] = jnp.zeros_like(n_ref)
            m_ref[...] = jnp.zeros_like(m_ref)

    head = pl.program_id(1)
    bias = bias_ref[...]
    nw = nw_ref[...]
    lane = lax.broadcasted_iota(jnp.int32, (L, GATE_LANES), 1)
    ti = lax.broadcasted_iota(jnp.int32, (L, L), 0)
    si = lax.broadcasted_iota(jnp.int32, (L, L), 1)
    eye = ti == si
    causal = si <= ti
    scale = HEAD_DIM_AB ** -0.5

    def unit(u, carry_val):
        rows = pl.ds(pl.multiple_of(u * L, L), L)
        if carry:
            c, n, m = c_ref[...], n_ref[...], m_ref[:, 0:1]
        else:
            c, n, m = c0_ref[u], n0_ref[u], m0_ref[u][:, 0:1]
        x = gates_ref[rows, :] + bias
        ig = jnp.sum(jnp.where(lane == head, x, 0.0), axis=1, keepdims=True)
        fpre = jnp.sum(jnp.where(lane == head + ML_HEADS, x, 0.0), axis=1, keepdims=True)
        lf = jnp.minimum(fpre, 0.0) - jnp.log1p(jnp.exp(-jnp.abs(fpre)))
        f_cum = jnp.sum(jnp.where(causal, _col_to_row(lf, eye), 0.0), axis=1, keepdims=True)
        a = ig - f_cum
        a_row = _col_to_row(a, eye)
        cmax = jnp.max(jnp.where(causal, a_row, -jnp.inf), axis=1, keepdims=True)
        m_t = f_cum + jnp.maximum(m, cmax)
        d_mat = jnp.exp(jnp.where(causal, (f_cum - m_t) + a_row, -jnp.inf))
        q = q_ref[rows, :]
        k = k_ref[rows, :] * scale
        v = v_ref[rows, :]
        s = _dot_nt(q, k) * d_mat
        inter = jnp.exp(f_cum + m - m_t)
        num = inter * _dot(q, c) + _dot(s, v)
        den = inter * jnp.sum(q * n, axis=1, keepdims=True) + jnp.sum(s, axis=1, keepdims=True)
        hh = num * (1.0 / jnp.maximum(jnp.abs(den), jnp.exp(-m_t)))
        f_end = f_cum[L - 1:L, :]
        m_end = m_t[L - 1:L, :]
        w = jnp.exp(f_end - m_end + a)
        cdec = jnp.exp(f_end + m - m_end)
        kw = k * w
        c_new = cdec * c + _dot_tn(kw, v)
        n_new = cdec * n + jnp.sum(kw, axis=0, keepdims=True)
        out = jax.nn.sigmoid(og_ref[rows, :]) * (_head_norm(hh) * nw)
        o_ref[rows, :] = out.astype(o_ref.dtype)
        m_new = jnp.broadcast_to(m_end, (1, GATE_LANES))
        if carry:
            c_ref[...] = c_new
            n_ref[...] = n_new
            m_ref[...] = m_new
        else:
            c_ref[u] = c_new
            n_ref[u] = n_new
            m_ref[u] = m_new
        return carry_val

    lax.fori_loop(0, units, unit, 0, unroll=unroll)


def _mlstm(proj, gates, bias, norm_w, mixed, *, row0, batch, seq, L, rows_per_step,
           c0=None, n0=None, m0=None, unroll=1):
    d = HEAD_DIM_AB
    h_ = ML_HEADS
    carry = c0 is None
    units = rows_per_step // L
    rb0 = row0 // rows_per_step
    col0 = 4 * RET_HEADS
    if carry:
        tblocks = seq // rows_per_step
        grid = (batch, h_, tblocks)
        row_idx = lambda b, h, t: rb0 + b * tblocks + t
        lead = None
    else:
        nb = rows_per_step // seq
        grid = (batch // nb, h_, 1)
        row_idx = lambda b, h, t: rb0 + b
        lead = nb
    c_spec = pl.BlockSpec((lead, None, d, d), lambda b, h, t: (b, h, 0, 0))
    n_spec = pl.BlockSpec((lead, None, 1, d), lambda b, h, t: (b, h, 0, 0))
    m_spec = pl.BlockSpec((lead, None, 1, GATE_LANES), lambda b, h, t: (b, h, 0, 0))

    def col_spec(c_blk):
        return pl.BlockSpec((rows_per_step, d), lambda b, h, t: (row_idx(b, h, t), c_blk + h))

    in_specs = [col_spec(col0), col_spec(col0 + h_), col_spec(col0 + 2 * h_), col_spec(col0 + 3 * h_),
                pl.BlockSpec((rows_per_step, GATE_LANES), lambda b, h, t: (row_idx(b, h, t), 0)),
                pl.BlockSpec((1, GATE_LANES), lambda b, h, t: (0, 0)),
                pl.BlockSpec((1, d), lambda b, h, t: (0, h))]
    args = [proj, proj, proj, proj, gates, bias, norm_w]
    if not carry:
        in_specs += [c_spec, n_spec, m_spec]
        args += [c0, n0, m0]
    aliases = _alias_existing(mixed, in_specs, args)
    return pl.pallas_call(
        functools.partial(_mlstm_kernel, L, units, carry, unroll),
        grid=grid,
        in_specs=in_specs,
        out_specs=[col_spec(RET_HEADS), c_spec, n_spec, m_spec],
        out_shape=[jax.ShapeDtypeStruct(mixed.shape, mixed.dtype),
                   jax.ShapeDtypeStruct((batch, h_, d, d), F32),
                   jax.ShapeDtypeStruct((batch, h_, 1, d), F32),
                   jax.ShapeDtypeStruct((batch, h_, 1, GATE_LANES), F32)],
        input_output_aliases=aliases,
        compiler_params=_cparams(("parallel", "parallel", "arbitrary")),
        name="mlstm_prompt" if carry else "mlstm_sample",
    )(*args)


def _hgrn_kernel(L, units, carry, unroll, *refs):
    (q_ref, f_ref, i_ref, g_ref, lbl_ref, nw_ref) = refs[:6]
    if carry:
        s0_ref = None
        o_ref, sout_ref, st_ref = refs[-3:]
    else:
        s0_ref = refs[6]
        o_ref, sout_ref = refs[-2:]
        st_ref = None

    if carry:
        @pl.when(pl.program_id(2) == 0)
        def _():
            st_ref[...] = jnp.zeros_like(st_ref)

    d = HG_DIM
    lg = lbl_ref[...]
    e = jnp.exp(lg - jnp.max(lg, axis=0, keepdims=True))
    sm = e / jnp.sum(e, axis=0, keepdims=True)
    lb = (sm[0:1, :] + sm[1:2, :]) - sm[0:1, :]
    nw = nw_ref[...]
    groups = L // 8
    pairs = [(j, jp) for jp in range(groups - 1) for j in range(jp + 1, groups)]
    row8 = lax.broadcasted_iota(jnp.int32, (8, d), 0)
    lane = lax.broadcasted_iota(jnp.int32, (8, L), 1)
    if not carry:
        eye_d = (lax.broadcasted_iota(jnp.int32, (d, d), 0)
                 == lax.broadcasted_iota(jnp.int32, (d, d), 1))

    def unit(u, carry_val):
        rows = pl.ds(pl.multiple_of(u * L, L), L)
        q = jax.nn.silu(q_ref[rows, :])
        fg = lb + (1.0 - lb) * jax.nn.sigmoid(f_ref[rows, :])
        k = 1.0 - fg
        logf = jnp.log(fg)
        v = i_ref[rows, :]
        gg = []
        base = jnp.zeros((1, d), F32)
        for j in range(groups):
            x = logf[8 * j:8 * j + 8, :]
            cum = jnp.zeros((8, d), F32)
            for r in range(8):
                cum = cum + jnp.where(row8 >= r, x[r:r + 1, :], 0.0)
            cum = cum + base
            gg.append(cum)
            base = cum[7:8, :]
        g_end = base
        g_cum = jnp.concatenate(gg, axis=0)
        qg = [q[8 * j:8 * j + 8, :] for j in range(groups)]
        kg = [k[8 * j:8 * j + 8, :] for j in range(groups)]
        att = []
        for j in range(groups):
            a = jnp.zeros((8, L), F32)
            for r in range(8):
                diff = jnp.where(row8 >= r, gg[j] - gg[j][r:r + 1, :], -jnp.inf)
                col = jnp.sum(qg[j] * kg[j][r:r + 1, :] * jnp.exp(diff), axis=1, keepdims=True)
                a = jnp.where(lane == 8 * j + r, col, a)
            att.append(a)
        if pairs:
            ends = [gg[j][7:8, :] for j in range(groups)]
            kt = jnp.concatenate([kg[j] * jnp.exp(ends[j] - gg[j]) for j in range(groups)], axis=0)
            qt = jnp.concatenate([qg[j] * jnp.exp(gg[j] - ends[jp]) for j, jp in pairs], axis=0)
            cross = _dot_nt(qt, kt)
            for idx, (j, jp) in enumerate(pairs):
                in_group = (lane >= 8 * jp) & (lane < 8 * jp + 8)
                att[j] = jnp.where(in_group, cross[8 * idx:8 * idx + 8, :], att[j])
        att_m = jnp.concatenate(att, axis=0)
        qs = q * jnp.exp(g_cum)
        kd = k * jnp.exp(g_end - g_cum)
        if carry:
            st = st_ref[...]
            o = _dot_nt(qs, st) + _dot(att_m, v)
            st_ref[...] = st * jnp.exp(g_end) + _dot_tn(v, kd)
        else:
            s = s0_ref[u]
            o = _dot(qs, s) + _dot(att_m, v)
            sout_ref[u] = _row_to_col(jnp.exp(g_end), eye_d) * s + _dot_tn(kd, v)
        out = (_head_norm(o) * nw) * jax.nn.silu(g_ref[rows, :])
        o_ref[rows, :] = out.astype(o_ref.dtype)
        return carry_val

    lax.fori_loop(0, units, unit, 0, unroll=unroll)

    if carry:
        @pl.when(pl.program_id(2) == pl.num_programs(2) - 1)
        def _():
            sout_ref[...] = st_ref[...].T


def _hgrn(proj, lb_logits, norm_w, mixed, *, row0, batch, seq, L, rows_per_step, s0=None, unroll=1):
    d = HG_DIM
    h_ = proj.shape[1] // (4 * d)
    carry = s0 is None
    units = rows_per_step // L
    rb0 = row0 // rows_per_step
    if carry:
        tblocks = seq // rows_per_step
        grid = (batch, h_, tblocks)
        row_idx = lambda b, h, t: rb0 + b * tblocks + t
        sout_spec = pl.BlockSpec((None, None, d, d), lambda b, h, t: (b, h, 0, 0))
        scratch = [pltpu.VMEM((d, d), F32)]
    else:
        nb = rows_per_step // seq
        grid = (batch // nb, h_, 1)
        row_idx = lambda b, h, t: rb0 + b
        sout_spec = pl.BlockSpec((nb, None, d, d), lambda b, h, t: (b, h, 0, 0))
        scratch = []

    def col_spec(c0):
        return pl.BlockSpec((rows_per_step, d), lambda b, h, t: (row_idx(b, h, t), c0 + h))

    in_specs = [col_spec(0), col_spec(h_), col_spec(2 * h_), col_spec(3 * h_),
                pl.BlockSpec((2, d), lambda b, h, t: (0, h)),
                pl.BlockSpec((1, d), lambda b, h, t: (0, 0))]
    args = [proj, proj, proj, proj, lb_logits, norm_w]
    if not carry:
        in_specs.append(sout_spec)
        args.append(s0)
    aliases = _alias_existing(mixed, in_specs, args)
    return pl.pallas_call(
        functools.partial(_hgrn_kernel, L, units, carry, unroll),
        grid=grid,
        in_specs=in_specs,
        out_specs=[col_spec(0), sout_spec],
        out_shape=[jax.ShapeDtypeStruct(mixed.shape, mixed.dtype),
                   jax.ShapeDtypeStruct((batch, h_, d, d), F32)],
        scratch_shapes=scratch,
        input_output_aliases=aliases,
        compiler_params=_cparams(("parallel", "parallel", "arbitrary")),
        name="hgrn_prompt" if carry else "hgrn_sample",
    )(*args)


def _chunk_len(t, c):
    return c if t % c == 0 else t


def _swiglu_residual(x, norm_w, w_gate, w_up, w_down, layer):
    hn = _rmsnorm(x, norm_w, BF16)
    act = _gate_up(hn, w_gate, w_up, layer, TM_WIDE, 256)
    return _matmul_residual(act, w_down, layer, x, TM_DOWN, 256, single_buffer_lhs=True,
                            name="down_proj")


def kernel(x_prompt, x_sample, state_ret, state_mlstm_C, state_mlstm_n, state_mlstm_m, state_hgrn,
           norm_mix_w, w_in_ab, b_if_ab, ml_norm_w, w_out_ab, w_in_c, lb_logits, hg_norm_w, w_out_c,
           norm_ffn_w, w_gate, w_up, w_down, norm_final_w):
    bp, tp, d = x_prompt.shape
    bs, ts, _ = x_sample.shape
    mp, ms = bp * tp, bs * ts
    tm = TM_WIDE

    w_in_ab_b, w_out_ab_b = w_in_ab.astype(BF16), w_out_ab
    w_in_c_b, w_out_c_b = w_in_c, w_out_c
    w_gate_b, w_up_b, w_down_b = w_gate, w_up, w_down

    ab_main = 4 * RET_HEADS * HEAD_DIM_AB + 4 * ML_HEADS * HEAD_DIM_AB
    x, hn = _rmsnorm_stack([x_prompt.reshape(mp, d), x_sample.reshape(ms, d)], norm_mix_w[0])
    proj = _matmul(hn, w_in_ab_b, 0, tm, 512, n_out=ab_main, name="in_proj_ab")
    gates = _matmul(hn, w_in_ab_b, 0, tm, GATE_LANES, n_out=GATE_LANES,
                    col_block0=ab_main // GATE_LANES, name="in_proj_gates")
    bias = jnp.pad(b_if_ab[0].astype(F32), (0, GATE_LANES - 2 * ML_HEADS)).reshape(1, GATE_LANES)
    ml_w = ml_norm_w[0].astype(F32).reshape(1, -1)

    mixed = jax.ShapeDtypeStruct((mp + ms, d), BF16)
    lp = _chunk_len(tp, RET_CHUNK)
    ls = _chunk_len(ts, RET_CHUNK)
    mixed, ret_p = _retention(proj, mixed, row0=0, batch=bp, seq=tp, L=lp, rows_per_step=4 * lp,
                              pos0=0.0, unroll=UNROLL_AB_PROMPT)
    mixed, ret_s = _retention(proj, mixed, row0=mp, batch=bs, seq=ts, L=ls, rows_per_step=8 * ls,
                              pos0=float(PAST_LEN), s0=state_ret[0], unroll=UNROLL_AB_SAMPLE)
    lp = _chunk_len(tp, ML_CHUNK)
    ls = _chunk_len(ts, ML_CHUNK)
    mixed, mc_p, mn_p, mm_p = _mlstm(proj, gates, bias, ml_w, mixed, row0=0, batch=bp, seq=tp, L=lp,
                                     rows_per_step=4 * lp, unroll=UNROLL_AB_PROMPT)
    m0 = jnp.broadcast_to(state_mlstm_m[0][:, :, None, None], (bs, ML_HEADS, 1, GATE_LANES))
    mixed, mc_s, mn_s, mm_s = _mlstm(proj, gates, bias, ml_w, mixed, row0=mp, batch=bs, seq=ts, L=ls,
                                     rows_per_step=8 * ls, c0=state_mlstm_C[0],
                                     n0=state_mlstm_n[0][:, :, None, :], m0=m0, unroll=UNROLL_AB_SAMPLE)
    x = _matmul_residual(mixed, w_out_ab_b, 0, x, tm, 512, name="out_proj_ab")
    x = _swiglu_residual(x, norm_ffn_w[0], w_gate_b, w_up_b, w_down_b, 0)

    hn = _rmsnorm(x, norm_mix_w[1], BF16)
    proj = _matmul(hn, w_in_c_b, 0, tm, 512, name="in_proj_c")
    mixed = jax.ShapeDtypeStruct((mp + ms, d), BF16)
    hg_w =hg_norm_w[0].astype(F32).reshape(1, -1)
    lbl = lb_logits.astype(F32)
    lp = _chunk_len(tp, HG_CHUNK)
    ls = _chunk_len(ts, HG_CHUNK)
    mixed, hg_p = _hgrn(proj, lbl, hg_w, mixed, row0=0, batch=bp, seq=tp, L=lp, rows_per_step=16 * lp,
                        unroll=UNROLL_HG_PROMPT)
    mixed, hg_s = _hgrn(proj, lbl, hg_w, mixed, row0=mp, batch=bs, seq=ts, L=ls, rows_per_step=16 * ls,
                        s0=state_hgrn[0], unroll=UNROLL_HG_SAMPLE)
    x = _matmul_residual(mixed, w_out_c_b, 0, x, tm, 512, name="out_proj_c")
    x = _swiglu_residual(x, norm_ffn_w[1], w_gate_b, w_up_b, w_down_b, 1)

    y_prompt = _rmsnorm(x, norm_final_w, F32, row0=0, rows=mp).reshape(bp, tp, d)
    y_sample = _rmsnorm(x, norm_final_w, F32, row0=mp, rows=ms).reshape(bs, ts, d)
    return (y_prompt, y_sample,
            ret_p[None], mc_p[None], mn_p[:, :, 0, :][None], mm_p[:, :, 0, 0][None], hg_p[None],
            ret_s[None], mc_s[None], mn_s[:, :, 0, :][None], mm_s[:, :, 0, 0][None], hg_s[None])
```

```python
import functools

import numpy as np
import jax
import jax.numpy as jnp
from jax import lax
from jax.experimental import pallas as pl
from jax.experimental.pallas import tpu as pltpu

F32 = jnp.float32
BF16 = jnp.bfloat16

EPS = 1e-6
ROPE_BASE = 10000.0
PAST_LEN = 16384

RET_HEADS = 8
ML_HEADS = 8
HEAD_DIM_AB = 256
HG_DIM = 128
RET_CHUNK = 128
ML_CHUNK = 128
HG_CHUNK = 32
LANES = 128
GATE_LANES = LANES

V7X_VMEM_LIMIT_BYTES = 56 * 1024 * 1024

TM_WIDE = 1536
TM_OUT = 1024
TM_DOWN = 1024

UNROLL_AB_PROMPT = 2
UNROLL_AB_SAMPLE = 8
UNROLL_HG_PROMPT = 32
UNROLL_HG_SAMPLE = 8


def _cparams(semantics):
    return pltpu.CompilerParams(dimension_semantics=semantics,
                                vmem_limit_bytes=V7X_VMEM_LIMIT_BYTES)


def _rmsnorm_kernel(x_ref, w_ref, o_ref):
    x = x_ref[...]
    y = x * lax.rsqrt(jnp.mean(x * x, axis=-1, keepdims=True) + EPS)
    o_ref[...] = (y * w_ref[...]).astype(o_ref.dtype)


def _rmsnorm(x, w, out_dtype, tm=512, row0=0, rows=None):
    d = x.shape[1]
    rows = x.shape[0] if rows is None else rows
    rb0 = row0 // tm
    return pl.pallas_call(
        _rmsnorm_kernel,
        grid=(rows // tm,),
        in_specs=[pl.BlockSpec((tm, d), lambda i: (rb0 + i, 0)),
                  pl.BlockSpec((1, d), lambda i: (0, 0))],
        out_specs=pl.BlockSpec((tm, d), lambda i: (i, 0)),
        out_shape=jax.ShapeDtypeStruct((rows, d), out_dtype),
        compiler_params=_cparams(("parallel",)),
        name="rmsnorm",
    )(x, w.reshape(1, d).astype(F32))


def _rmsnorm_stack_kernel(x_ref, w_ref, *rest):
    xo_ref, ho_ref = rest[-2:]
    x = x_ref[...]
    xo_ref[...] = x
    y = x * lax.rsqrt(jnp.mean(x * x, axis=-1, keepdims=True) + EPS)
    ho_ref[...] = (y * w_ref[...]).astype(ho_ref.dtype)


def _rmsnorm_stack(parts, w, tm=512):
    d = parts[0].shape[1]
    m = sum(p.shape[0] for p in parts)
    out_shape = [jax.ShapeDtypeStruct((m, d), F32), jax.ShapeDtypeStruct((m, d), BF16)]
    w2 = w.reshape(1, d).astype(F32)
    outs = None
    row0 = 0
    for part in parts:
        rb0 = row0 // tm
        out_spec = pl.BlockSpec((tm, d), lambda i, rb0=rb0: (rb0 + i, 0))
        in_specs = [pl.BlockSpec((tm, d), lambda i: (i, 0)), pl.BlockSpec((1, d), lambda i: (0, 0))]
        args = [part, w2]
        aliases = {}
        if outs is not None:
            in_specs += [pl.BlockSpec(memory_space=pl.ANY)] * 2
            args += list(outs)
            aliases = {2: 0, 3: 1}
        outs = pl.pallas_call(
            _rmsnorm_stack_kernel,
            grid=(part.shape[0] // tm,),
            in_specs=in_specs,
            out_specs=[out_spec, out_spec],
            out_shape=out_shape,
            input_output_aliases=aliases,
            compiler_params=_cparams(("parallel",)),
            name="rmsnorm_stack",
        )(*args)
        row0 += part.shape[0]
    return outs


def _mm_kernel(x_ref, w_ref, o_ref):
    w = w_ref[...].astype(BF16)
    o_ref[...] = jnp.dot(x_ref[...], w, preferred_element_type=F32).astype(o_ref.dtype)


def _mm_residual_kernel(x_ref, w_ref, r_ref, o_ref):
    w = w_ref[...].astype(BF16)
    o_ref[...] = r_ref[...] + jnp.dot(x_ref[...], w, preferred_element_type=F32)


def _gate_up_kernel(x_ref, wg_ref, wu_ref, o_ref):
    x = x_ref[...]
    a = jnp.dot(x, wg_ref[...].astype(BF16), preferred_element_type=F32)
    b = jnp.dot(x, wu_ref[...].astype(BF16), preferred_element_type=F32)
    o_ref[...] = (jax.nn.silu(a) * b).astype(o_ref.dtype)


def _mm_residual_prenorm_kernel(x_ref, w_ref, r_ref, g_ref, o_ref, yg_ref, ssq_ref):
    w = w_ref[...].astype(BF16)
    y = r_ref[...] + jnp.dot(x_ref[...], w, preferred_element_type=F32)
    o_ref[...] = y
    yg_ref[...] = (y * g_ref[...]).astype(yg_ref.dtype)
    sq = y * y
    part = sq[:, :LANES]
    for c in range(1, sq.shape[1] // LANES):
        part = part + sq[:, c * LANES:(c + 1) * LANES]

    @pl.when(pl.program_id(1) == 0)
    def _():
        ssq_ref[...] = part

    @pl.when(pl.program_id(1) > 0)
    def _():
        ssq_ref[...] += part


def _rms_scale_kernel(d_model, ssq_ref, o_ref):
    ms = jnp.sum(ssq_ref[...], axis=-1, keepdims=True) / d_model
    o_ref[...] = jnp.broadcast_to(lax.rsqrt(ms + EPS), o_ref.shape)


def _rms_scale(ssq, d_model, tm=1024):
    m = ssq.shape[0]
    spec = pl.BlockSpec((tm, LANES), lambda i: (i, 0))
    return pl.pallas_call(
        functools.partial(_rms_scale_kernel, d_model),
        grid=(m // tm,),
        in_specs=[spec],
        out_specs=spec,
        out_shape=jax.ShapeDtypeStruct((m, LANES), F32),
        compiler_params=_cparams(("parallel",)),
        name="rms_scale",
    )(ssq)


def _lane_tiled(scale_ref, width):
    r = scale_ref[...]
    return jnp.concatenate([r] * (width // LANES), axis=1)


def _mm_postnorm_kernel(x_ref, w_ref, scale_ref, o_ref):
    r = _lane_tiled(scale_ref, o_ref.shape[1])
    w = w_ref[...].astype(BF16)
    acc = jnp.dot(x_ref[...], w, preferred_element_type=F32)
    o_ref[...] = (r * acc).astype(o_ref.dtype)


def _gate_up_postnorm_kernel(x_ref, wg_ref, wu_ref, scale_ref, o_ref):
    r = _lane_tiled(scale_ref, o_ref.shape[1])
    x = x_ref[...]
    a = r * jnp.dot(x, wg_ref[...].astype(BF16), preferred_element_type=F32)
    b = r * jnp.dot(x, wu_ref[...].astype(BF16), preferred_element_type=F32)
    o_ref[...] = (jax.nn.silu(a) * b).astype(o_ref.dtype)


def _weight_spec(w, layer, tn, col_block0=0):
    return pl.BlockSpec((None, w.shape[1], tn), lambda i, j: (layer, 0, col_block0 + j))


def _matmul(x, w, layer, tm, tn, n_out=None, col_block0=0, scale=None, out_dtype=F32, name="matmul"):
    m, k = x.shape
    n = w.shape[2] if n_out is None else n_out
    in_specs = [pl.BlockSpec((tm, k), lambda i, j: (i, 0)),
                _weight_spec(w, layer, tn, col_block0)]
    args = [x, w]
    body = _mm_kernel
    if scale is not None:
        in_specs.append(pl.BlockSpec((tm, LANES), lambda i, j: (i, 0)))
        args.append(scale)
        body = _mm_postnorm_kernel
    return pl.pallas_call(
        body,
        grid=(m // tm, n // tn),
        in_specs=in_specs,
        out_specs=pl.BlockSpec((tm, tn), lambda i, j: (i, j)),
        out_shape=jax.ShapeDtypeStruct((m, n), out_dtype),
        compiler_params=_cparams(("parallel", "arbitrary")),
        name=name,
    )(*args)


def _matmul_residual(x, w, layer, res, tm, tn, single_buffer_lhs=False, next_norm_w=None,
                     name="matmul_residual"):
    m, k = x.shape
    n = w.shape[2]
    lhs_kwargs = {"pipeline_mode": pl.Buffered(1)} if single_buffer_lhs else {}
    tile = pl.BlockSpec((tm, tn), lambda i, j: (i, j))
    in_specs = [pl.BlockSpec((tm, k), lambda i, j: (i, 0), **lhs_kwargs),
                _weight_spec(w, layer, tn), tile]
    args = [x, w, res]
    if next_norm_w is None:
        body, out_specs, out_shape = _mm_residual_kernel, tile, jax.ShapeDtypeStruct((m, n), F32)
    else:
        in_specs.append(pl.BlockSpec((1, tn), lambda i, j: (0, j)))
        args.append(next_norm_w.reshape(1, n).astype(F32))
        body = _mm_residual_prenorm_kernel
        out_specs = [tile, tile, pl.BlockSpec((tm, LANES), lambda i, j: (i, 0))]
        out_shape = [jax.ShapeDtypeStruct((m, n), F32), jax.ShapeDtypeStruct((m, n), BF16),
                     jax.ShapeDtypeStruct((m, LANES), F32)]
    return pl.pallas_call(
        body,
        grid=(m // tm, n // tn),
        in_specs=in_specs,
        out_specs=out_specs,
        out_shape=out_shape,
        compiler_params=_cparams(("parallel", "arbitrary")),
        name=name,
    )(*args)


def _gate_up(x, wg, wu, layer, tm, tn, scale):
    m, k = x.shape
    n = wg.shape[2]
    return pl.pallas_call(
        _gate_up_postnorm_kernel,
        grid=(m // tm, n // tn),
        in_specs=[pl.BlockSpec((tm, k), lambda i, j: (i, 0)),
                  _weight_spec(wg, layer, tn),
                  _weight_spec(wu, layer, tn),
                  pl.BlockSpec((tm, LANES), lambda i, j: (i, 0))],
        out_specs=pl.BlockSpec((tm, tn), lambda i, j: (i, j)),
        out_shape=jax.ShapeDtypeStruct((m, n), BF16),
        compiler_params=_cparams(("parallel", "arbitrary")),
        name="gate_up",
    )(x, wg, wu, scale)


def _dot_nt(a, b):
    return lax.dot_general(a, b, (((1,), (1,)), ((), ())), preferred_element_type=F32)


def _dot_tn(a, b):
    return lax.dot_general(a, b, (((0,), (0,)), ((), ())), preferred_element_type=F32)


def _dot(a, b):
    return jnp.dot(a, b, preferred_element_type=F32)


def _head_norm(x):
    return x * lax.rsqrt(jnp.mean(x * x, axis=-1, keepdims=True) + EPS)


def _col_to_row(col, eye):
    return jnp.sum(jnp.where(eye, col, 0.0), axis=0, keepdims=True)


def _row_to_col(row, eye):
    return jnp.sum(jnp.where(eye, row, 0.0), axis=1, keepdims=True)


def _alias_existing(mixed, in_specs, args):
    if isinstance(mixed, jax.ShapeDtypeStruct):
        return {}
    in_specs.append(pl.BlockSpec(memory_space=pl.ANY))
    args.append(mixed)
    return {len(args) - 1: 0}


def _rotary(x, cos, sin):
    half = x.shape[-1] // 2
    x1, x2 = x[:, :half], x[:, half:]
    return jnp.concatenate([x1 * cos - x2 * sin, x1 * sin + x2 * cos], axis=-1)


def _ret_kernel(L, units, carry, unroll, *refs):
    (q_ref, k_ref, v_ref, g_ref, cos_ref, sin_ref, dec_ref, qd_ref, kd_ref, cd_ref) = refs[:10]
    s0_ref = None if carry else refs[10]
    o_ref, sout_ref = refs[-2:]

    if carry:
        @pl.when(pl.program_id(2) == 0)
        def _():
            sout_ref[...] = jnp.zeros_like(sout_ref)

    dec = dec_ref[...]
    qd = qd_ref[...]
    kd = kd_ref[...]
    cd = cd_ref[...]
    scale = HEAD_DIM_AB ** -0.5

    def unit(u, c):
        rows = pl.ds(pl.multiple_of(u * L, L), L)
        if carry:
            cos, sin = cos_ref[rows, :], sin_ref[rows, :]
            s = sout_ref[...]
        else:
            cos, sin = cos_ref[...], sin_ref[...]
            s = s0_ref[u]
        q = _rotary(q_ref[rows, :], cos, sin)
        k = _rotary(k_ref[rows, :], cos, sin) * scale
        v = v_ref[rows, :]
        att = _dot_nt(q, k) * dec
        o = _dot(att, v) + _dot(q, s) * qd
        s_new = cd * s + _dot_tn(k * kd, v)
        out = jax.nn.silu(g_ref[rows, :]) * _head_norm(o)
        o_ref[rows, :] = out.astype(o_ref.dtype)
        if carry:
            sout_ref[...] = s_new
        else:
            sout_ref[u] = s_new
        return c

    lax.fori_loop(0, units, unit, 0, unroll=unroll)


def _retention_tables(L):
    log_gamma = jnp.log1p(-jnp.exp2(-5.0 - jnp.arange(RET_HEADS, dtype=F32)))[:, None]
    idx = jnp.arange(L, dtype=F32)
    diff = idx[:, None] - idx[None, :]
    decay = jnp.exp(jnp.where(diff >= 0, log_gamma[:, :, None] * diff, -jnp.inf))
    q_decay = jnp.exp(log_gamma * (idx + 1.0))[:, :, None]
    k_decay = jnp.exp(log_gamma * (L - 1.0 - idx))[:, :, None]
    chunk_decay = jnp.exp(log_gamma[:, 0] * L)[:, None, None]
    return decay, q_decay, k_decay, chunk_decay


def _rope_tables(pos):
    half = HEAD_DIM_AB // 2
    inv_freq = ROPE_BASE ** (-jnp.arange(half, dtype=F32) / half)
    ang = pos[:, None] * inv_freq[None, :]
    return jnp.cos(ang), jnp.sin(ang)


def _retention(proj, mixed, *, row0, batch, seq, L, rows_per_step, pos0, s0=None, unroll=1):
    d = HEAD_DIM_AB
    h_ = RET_HEADS
    carry = s0 is None
    units = rows_per_step // L
    decay, q_decay, k_decay, chunk_decay = _retention_tables(L)
    cos, sin = _rope_tables(pos0 + jnp.arange(seq, dtype=F32))
    rb0 = row0 // rows_per_step
    if carry:
        tblocks = seq // rows_per_step
        grid = (batch, h_, tblocks)
        row_idx = lambda b, h, t: rb0 + b * tblocks + t
        rope_spec = pl.BlockSpec((rows_per_step, d // 2), lambda b, h, t: (t, 0))
        sout_spec = pl.BlockSpec((None, None, d, d), lambda b, h, t: (b, h, 0, 0))
    else:
        nb = rows_per_step // seq
        grid = (batch // nb, h_, 1)
        row_idx = lambda b, h, t: rb0 + b
        rope_spec = pl.BlockSpec((seq, d // 2), lambda b, h, t: (0, 0))
        sout_spec = pl.BlockSpec((nb, None, d, d), lambda b, h, t: (b, h, 0, 0))

    def col_spec(c0):
        return pl.BlockSpec((rows_per_step, d), lambda b, h, t: (row_idx(b, h, t), c0 + h))

    in_specs = [col_spec(0), col_spec(h_), col_spec(2 * h_), col_spec(3 * h_),
                rope_spec, rope_spec,
                pl.BlockSpec((None, L, L), lambda b, h, t: (h, 0, 0)),
                pl.BlockSpec((None, L, 1), lambda b, h, t: (h, 0, 0)),
                pl.BlockSpec((None, L, 1), lambda b, h, t: (h, 0, 0)),
                pl.BlockSpec((None, 1, 1), lambda b, h, t: (h, 0, 0))]
    args = [proj, proj, proj, proj, cos, sin, decay, q_decay, k_decay, chunk_decay]
    if not carry:
        in_specs.append(sout_spec)
        args.append(s0)
    aliases = _alias_existing(mixed, in_specs, args)
    return pl.pallas_call(
        functools.partial(_ret_kernel, L, units, carry, unroll),
        grid=grid,
        in_specs=in_specs,
        out_specs=[col_spec(0), sout_spec],
        out_shape=[jax.ShapeDtypeStruct(mixed.shape, mixed.dtype),
                   jax.ShapeDtypeStruct((batch, h_, d, d), F32)],
        input_output_aliases=aliases,
        compiler_params=_cparams(("parallel", "parallel", "arbitrary")),
        name="retention_prompt" if carry else "retention_sample",
    )(*args)


def _mlstm_kernel(L, units, carry, unroll, *refs):
    (q_ref, k_ref, v_ref, og_ref, gates_ref, bias_ref, nw_ref) = refs[:7]
    c0_ref, n0_ref, m0_ref = (None, None, None) if carry else refs[7:10]
    o_ref, c_ref, n_ref, m_ref = refs[-4:]

    if carry:
        @pl.when(pl.program_id(2) == 0)
        def _():
            c_ref[...] = jnp.zeros_like(c_ref)
            n_ref[...] = jnp.zeros_like(n_ref)
            m_ref[...] = jnp.zeros_like(m_ref)

    head = pl.program_id(1)
    bias = bias_ref[...]
    nw = nw_ref[...]
    lane = lax.broadcasted_iota(jnp.int32, (L, GATE_LANES), 1)
    ti = lax.broadcasted_iota(jnp.int32, (L, L), 0)
    si = lax.broadcasted_iota(jnp.int32, (L, L), 1)
    eye = ti == si
    causal = si <= ti
    scale = HEAD_DIM_AB ** -0.5

    def unit(u, carry_val):
        rows = pl.ds(pl.multiple_of(u * L, L), L)
        if carry:
            c, n, m = c_ref[...], n_ref[...], m_ref[:, 0:1]
        else:
            c, n, m = c0_ref[u], n0_ref[u], m0_ref[u][:, 0:1]
        x = gates_ref[rows, :] + bias
        ig = jnp.sum(jnp.where(lane == head, x, 0.0), axis=1, keepdims=True)
        fpre = jnp.sum(jnp.where(lane == head + ML_HEADS, x, 0.0), axis=1, keepdims=True)
        lf = jnp.minimum(fpre, 0.0) - jnp.log1p(jnp.exp(-jnp.abs(fpre)))
        f_cum = jnp.sum(jnp.where(causal, _col_to_row(lf, eye), 0.0), axis=1, keepdims=True)
        a = ig - f_cum
        a_row = _col_to_row(a, eye)
        cmax = jnp.max(jnp.where(causal, a_row, -jnp.inf), axis=1, keepdims=True)
        m_t = f_cum + jnp.maximum(m, cmax)
        d_mat = jnp.exp(jnp.where(causal, (f_cum - m_t) + a_row, -jnp.inf))
        q = q_ref[rows, :]
        k = k_ref[rows, :] * scale
        v = v_ref[rows, :]
        s = _dot_nt(q, k) * d_mat
        inter = jnp.exp(f_cum + m - m_t)
        num = inter * _dot(q, c) + _dot(s, v)
        den = inter * jnp.sum(q * n, axis=1, keepdims=True) + jnp.sum(s, axis=1, keepdims=True)
        hh = num * (1.0 / jnp.maximum(jnp.abs(den), jnp.exp(-m_t)))
        f_end = f_cum[L - 1:L, :]
        m_end = m_t[L - 1:L, :]
        w = jnp.exp(f_end - m_end + a)
        cdec = jnp.exp(f_end + m - m_end)
        kw = k * w
        c_new = cdec * c + _dot_tn(kw, v)
        n_new = cdec * n + jnp.sum(kw, axis=0, keepdims=True)
        out = jax.nn.sigmoid(og_ref[rows, :]) * (_head_norm(hh) * nw)
        o_ref[rows, :] = out.astype(o_ref.dtype)
        m_new = jnp.broadcast_to(m_end, (1, GATE_LANES))
        if carry:
            c_ref[...] = c_new
            n_ref[...] = n_new
            m_ref[...] = m_new
        else:
            c_ref[u] = c_new
            n_ref[u] = n_new
            m_ref[u] = m_new
        return carry_val

    lax.fori_loop(0, units, unit, 0, unroll=unroll)


def _mlstm(proj, gates, bias, norm_w, mixed, *, row0, batch, seq, L, rows_per_step,
           c0=None, n0=None, m0=None, unroll=1):
    d = HEAD_DIM_AB
    h_ = ML_HEADS
    carry = c0 is None
    units = rows_per_step // L
    rb0 = row0 // rows_per_step
    col0 = 4 * RET_HEADS
    if carry:
        tblocks = seq // rows_per_step
        grid = (batch, h_, tblocks)
        row_idx = lambda b, h, t: rb0 + b * tblocks + t
        lead = None
    else:
        nb = rows_per_step // seq
        grid = (batch // nb, h_, 1)
        row_idx = lambda b, h, t: rb0 + b
        lead = nb
    c_spec = pl.BlockSpec((lead, None, d, d), lambda b, h, t: (b, h, 0, 0))
    n_spec = pl.BlockSpec((lead, None, 1, d), lambda b, h, t: (b, h, 0, 0))
    m_spec = pl.BlockSpec((lead, None, 1, GATE_LANES), lambda b, h, t: (b, h, 0, 0))

    def col_spec(c_blk):
        return pl.BlockSpec((rows_per_step, d), lambda b, h, t: (row_idx(b, h, t), c_blk + h))

    in_specs = [col_spec(col0), col_spec(col0 + h_), col_spec(col0 + 2 * h_), col_spec(col0 + 3 * h_),
                pl.BlockSpec((rows_per_step, GATE_LANES), lambda b, h, t: (row_idx(b, h, t), 0)),
                pl.BlockSpec((1, GATE_LANES), lambda b, h, t: (0, 0)),
                pl.BlockSpec((1, d), lambda b, h, t: (0, h))]
    args = [proj, proj, proj, proj, gates, bias, norm_w]
    if not carry:
        in_specs += [c_spec, n_spec, m_spec]
        args += [c0, n0, m0]
    aliases = _alias_existing(mixed, in_specs, args)
    return pl.pallas_call(
        functools.partial(_mlstm_kernel, L, units, carry, unroll),
        grid=grid,
        in_specs=in_specs,
        out_specs=[col_spec(RET_HEADS), c_spec, n_spec, m_spec],
        out_shape=[jax.ShapeDtypeStruct(mixed.shape, mixed.dtype),
                   jax.ShapeDtypeStruct((batch, h_, d, d), F32),
                   jax.ShapeDtypeStruct((batch, h_, 1, d), F32),
                   jax.ShapeDtypeStruct((batch, h_, 1, GATE_LANES), F32)],
        input_output_aliases=aliases,
        compiler_params=_cparams(("parallel", "parallel", "arbitrary")),
        name="mlstm_prompt" if carry else "mlstm_sample",
    )(*args)


def _hgrn_kernel(L, units, carry, unroll, *refs):
    (q_ref, f_ref, i_ref, g_ref, lbl_ref, nw_ref) = refs[:6]
    if carry:
        s0_ref = None
        o_ref, sout_ref, st_ref = refs[-3:]
    else:
        s0_ref = refs[6]
        o_ref, sout_ref = refs[-2:]
        st_ref = None

    if carry:
        @pl.when(pl.program_id(2) == 0)
        def _():
            st_ref[...] = jnp.zeros_like(st_ref)

    d = HG_DIM
    lg = lbl_ref[...]
    e = jnp.exp(lg - jnp.max(lg, axis=0, keepdims=True))
    sm = e / jnp.sum(e, axis=0, keepdims=True)
    lb = (sm[0:1, :] + sm[1:2, :]) - sm[0:1, :]
    nw = nw_ref[...]
    groups = L // 8
    pairs = [(j, jp) for jp in range(groups - 1) for j in range(jp + 1, groups)]
    row8 = lax.broadcasted_iota(jnp.int32, (8, d), 0)
    lane = lax.broadcasted_iota(jnp.int32, (8, L), 1)
    if not carry:
        eye_d = (lax.broadcasted_iota(jnp.int32, (d, d), 0)
                 == lax.broadcasted_iota(jnp.int32, (d, d), 1))

    def unit(u, carry_val):
        rows = pl.ds(pl.multiple_of(u * L, L), L)
        q = jax.nn.silu(q_ref[rows, :])
        fg = lb + (1.0 - lb) * jax.nn.sigmoid(f_ref[rows, :])
        k = 1.0 - fg
        logf = jnp.log(fg)
        v = i_ref[rows, :]
        gg = []
        base = jnp.zeros((1, d), F32)
        for j in range(groups):
            x = logf[8 * j:8 * j + 8, :]
            cum = jnp.zeros((8, d), F32)
            for r in range(8):
                cum = cum + jnp.where(row8 >= r, x[r:r + 1, :], 0.0)
            cum = cum + base
            gg.append(cum)
            base = cum[7:8, :]
        g_end = base
        g_cum = jnp.concatenate(gg, axis=0)
        qg = [q[8 * j:8 * j + 8, :] for j in range(groups)]
        kg = [k[8 * j:8 * j + 8, :] for j in range(groups)]
        att = []
        for j in range(groups):
            a = jnp.zeros((8, L), F32)
            for r in range(8):
                diff = jnp.where(row8 >= r, gg[j] - gg[j][r:r + 1, :], -jnp.inf)
                col = jnp.sum(qg[j] * kg[j][r:r + 1, :] * jnp.exp(diff), axis=1, keepdims=True)
                a = jnp.where(lane == 8 * j + r, col, a)
            att.append(a)
        if pairs:
            ends = [gg[j][7:8, :] for j in range(groups)]
            kt = jnp.concatenate([kg[j] * jnp.exp(ends[j] - gg[j]) for j in range(groups)], axis=0)
            qt = jnp.concatenate([qg[j] * jnp.exp(gg[j] - ends[jp]) for j, jp in pairs], axis=0)
            cross = _dot_nt(qt, kt)
            for idx, (j, jp) in enumerate(pairs):
                in_group = (lane >= 8 * jp) & (lane < 8 * jp + 8)
                att[j] = jnp.where(in_group, cross[8 * idx:8 * idx + 8, :], att[j])
        att_m = jnp.concatenate(att, axis=0)
        qs = q * jnp.exp(g_cum)
        kd = k * jnp.exp(g_end - g_cum)
        if carry:
            st = st_ref[...]
            o = _dot_nt(qs, st) + _dot(att_m, v)
            st_ref[...] = st * jnp.exp(g_end) + _dot_tn(v, kd)
        else:
            s = s0_ref[u]
            o = _dot(qs, s) + _dot(att_m, v)
            sout_ref[u] = _row_to_col(jnp.exp(g_end), eye_d) * s + _dot_tn(kd, v)
        out = (_head_norm(o) * nw) * jax.nn.silu(g_ref[rows, :])
        o_ref[rows, :] = out.astype(o_ref.dtype)
        return carry_val

    lax.fori_loop(0, units, unit, 0, unroll=unroll)

    if carry:
        @pl.when(pl.program_id(2) == pl.num_programs(2) - 1)
        def _():
            sout_ref[...] = st_ref[...].T


def _hgrn(proj, lb_logits, norm_w, mixed, *, row0, batch, seq, L, rows_per_step, s0=None, unroll=1):
    d = HG_DIM
    h_ = proj.shape[1] // (4 * d)
    carry = s0 is None
    units = rows_per_step // L
    rb0 = row0 // rows_per_step
    if carry:
        tblocks = seq // rows_per_step
        grid = (batch, h_, tblocks)
        row_idx = lambda b, h, t: rb0 + b * tblocks + t
        sout_spec = pl.BlockSpec((None, None, d, d), lambda b, h, t: (b, h, 0, 0))
        scratch = [pltpu.VMEM((d, d), F32)]
    else:
        nb = rows_per_step // seq
        grid = (batch // nb, h_, 1)
        row_idx = lambda b, h, t: rb0 + b
        sout_spec = pl.BlockSpec((nb, None, d, d), lambda b, h, t: (b, h, 0, 0))
        scratch = []

    def col_spec(c0):
        return pl.BlockSpec((rows_per_step, d), lambda b, h, t: (row_idx(b, h, t), c0 + h))

    in_specs = [col_spec(0), col_spec(h_), col_spec(2 * h_), col_spec(3 * h_),
                pl.BlockSpec((2, d), lambda b, h, t: (0, h)),
                pl.BlockSpec((1, d), lambda b, h, t: (0, 0))]
    args = [proj, proj, proj, proj, lb_logits, norm_w]
    if not carry:
        in_specs.append(sout_spec)
        args.append(s0)
    aliases = _alias_existing(mixed, in_specs, args)
    return pl.pallas_call(
        functools.partial(_hgrn_kernel, L, units, carry, unroll),
        grid=grid,
        in_specs=in_specs,
        out_specs=[col_spec(0), sout_spec],
        out_shape=[jax.ShapeDtypeStruct(mixed.shape, mixed.dtype),
                   jax.ShapeDtypeStruct((batch, h_, d, d), F32)],
        scratch_shapes=scratch,
        input_output_aliases=aliases,
        compiler_params=_cparams(("parallel", "parallel", "arbitrary")),
        name="hgrn_prompt" if carry else "hgrn_sample",
    )(*args)


def _chunk_len(t, c):
    return c if t % c == 0 else t


def _swiglu_residual(x, xg, ssq, w_gate, w_up, w_down, layer, next_norm_w):
    act = _gate_up(xg, w_gate, w_up, layer, TM_WIDE, 256, _rms_scale(ssq, x.shape[1]))
    return _matmul_residual(act, w_down, layer, x, TM_DOWN, 256, single_buffer_lhs=True,
                            next_norm_w=next_norm_w, name="down_proj")


def kernel(x_prompt, x_sample, state_ret, state_mlstm_C, state_mlstm_n, state_mlstm_m, state_hgrn,
           norm_mix_w, w_in_ab, b_if_ab, ml_norm_w, w_out_ab, w_in_c, lb_logits, hg_norm_w, w_out_c,
           norm_ffn_w, w_gate, w_up, w_down, norm_final_w):
    bp, tp, d = x_prompt.shape
    bs, ts, _ = x_sample.shape
    mp, ms = bp * tp, bs * ts
    tm = TM_WIDE

    w_in_ab_b, w_out_ab_b = w_in_ab.astype(BF16), w_out_ab
    w_in_c_b, w_out_c_b = w_in_c, w_out_c
    w_gate_b, w_up_b, w_down_b = w_gate, w_up, w_down

    ab_main = 4 * RET_HEADS * HEAD_DIM_AB + 4 * ML_HEADS * HEAD_DIM_AB
    x, hn = _rmsnorm_stack([x_prompt.reshape(mp, d), x_sample.reshape(ms, d)], norm_mix_w[0])
    proj = _matmul(hn, w_in_ab_b, 0, tm, 512, n_out=ab_main, name="in_proj_ab")
    gates = _matmul(hn, w_in_ab_b, 0, tm, GATE_LANES, n_out=GATE_LANES,
                    col_block0=ab_main // GATE_LANES, name="in_proj_gates")
    bias = jnp.pad(b_if_ab[0].astype(F32), (0, GATE_LANES - 2 * ML_HEADS)).reshape(1, GATE_LANES)
    ml_w = ml_norm_w[0].astype(F32).reshape(1, -1)

    mixed = jax.ShapeDtypeStruct((mp + ms, d), BF16)
    lp = _chunk_len(tp, RET_CHUNK)
    ls = _chunk_len(ts, RET_CHUNK)
    mixed, ret_p = _retention(proj, mixed, row0=0, batch=bp, seq=tp, L=lp, rows_per_step=4 * lp,
                              pos0=0.0, unroll=UNROLL_AB_PROMPT)
    mixed, ret_s = _retention(proj, mixed, row0=mp, batch=bs, seq=ts, L=ls, rows_per_step=8 * ls,
                              pos0=float(PAST_LEN), s0=state_ret[0], unroll=UNROLL_AB_SAMPLE)
    lp = _chunk_len(tp, ML_CHUNK)
    ls = _chunk_len(ts, ML_CHUNK)
    mixed, mc_p, mn_p, mm_p = _mlstm(proj, gates, bias, ml_w, mixed, row0=0, batch=bp, seq=tp, L=lp,
                                     rows_per_step=4 * lp, unroll=UNROLL_AB_PROMPT)
    m0 = jnp.broadcast_to(state_mlstm_m[0][:, :, None, None], (bs, ML_HEADS, 1, GATE_LANES))
    mixed, mc_s, mn_s, mm_s = _mlstm(proj, gates, bias, ml_w, mixed, row0=mp, batch=bs, seq=ts, L=ls,
                                     rows_per_step=8 * ls, c0=state_mlstm_C[0],
                                     n0=state_mlstm_n[0][:, :, None, :], m0=m0, unroll=UNROLL_AB_SAMPLE)
    x, xg, ssq = _matmul_residual(mixed, w_out_ab_b, 0, x, TM_OUT, 512, next_norm_w=norm_ffn_w[0],
                                  name="out_proj_ab")
    x, xg, ssq = _swiglu_residual(x, xg, ssq, w_gate_b, w_up_b, w_down_b, 0, norm_mix_w[1])

    proj = _matmul(xg, w_in_c_b, 0, tm, 512, scale=_rms_scale(ssq, d), name="in_proj_c")
    mixed = jax.ShapeDtypeStruct((mp + ms, d), BF16)
    hg_w =hg_norm_w[0].astype(F32).reshape(1, -1)
    lbl = lb_logits.astype(F32)
    lp = _chunk_len(tp, HG_CHUNK)
    ls = _chunk_len(ts, HG_CHUNK)
    mixed, hg_p = _hgrn(proj, lbl, hg_w, mixed, row0=0, batch=bp, seq=tp, L=lp, rows_per_step=32 * lp,
                        unroll=UNROLL_HG_PROMPT)
    mixed, hg_s = _hgrn(proj, lbl, hg_w, mixed, row0=mp, batch=bs, seq=ts, L=ls, rows_per_step=32 * ls,
                        s0=state_hgrn[0], unroll=UNROLL_HG_SAMPLE)
    x, xg, ssq = _matmul_residual(mixed, w_out_c_b, 0, x, TM_OUT, 512, next_norm_w=norm_ffn_w[1],
                                  name="out_proj_c")
    x = _swiglu_residual(x, xg, ssq, w_gate_b, w_up_b, w_down_b, 1, None)

    y_prompt = _rmsnorm(x, norm_final_w, F32, row0=0, rows=mp).reshape(bp, tp, d)
    y_sample = _rmsnorm(x, norm_final_w, F32, row0=mp, rows=ms).reshape(bs, ts, d)
    return (y_prompt, y_sample,
            ret_p[None], mc_p[None], mn_p[:, :, 0, :][None], mm_p[:, :, 0, 0][None], hg_p[None],
            ret_s[None], mc_s[None], mn_s[:, :, 0, :][None], mm_s[:, :, 0, 0][None], hg_s[None])
```

```python
import functools

import numpy as np
import jax
import jax.numpy as jnp
from jax import lax
from jax.experimental import pallas as pl
from jax.experimental.pallas import tpu as pltpu

F32 = jnp.float32
BF16 = jnp.bfloat16

EPS = 1e-6
ROPE_BASE = 10000.0
PAST_LEN = 16384

RET_HEADS = 8
ML_HEADS = 8
HEAD_DIM_AB = 256
HG_DIM = 128
RET_CHUNK = 128
ML_CHUNK = 128
HG_CHUNK = 32
LANES = 128
GATE_LANES = LANES

V7X_VMEM_LIMIT_BYTES = 56 * 1024 * 1024

TM_WIDE = 1536
TM_GATE_UP = TM_WIDE
TM_OUT = 1024
TM_DOWN = 1024

UNROLL_AB_PROMPT = 2
UNROLL_AB_SAMPLE = 8
UNROLL_HG_PROMPT = 32
UNROLL_HG_SAMPLE = 8


def _cparams(semantics, **kwargs):
    return pltpu.CompilerParams(dimension_semantics=semantics,
                                vmem_limit_bytes=V7X_VMEM_LIMIT_BYTES, **kwargs)


def _recurrence_cparams():
    return _cparams(("parallel", "parallel", "arbitrary"))


def _rmsnorm_kernel(x_ref, w_ref, o_ref):
    x = x_ref[...]
    y = x * lax.rsqrt(jnp.mean(x * x, axis=-1, keepdims=True) + EPS)
    o_ref[...] = (y * w_ref[...]).astype(o_ref.dtype)


def _rmsnorm(x, w, out_dtype, tm=512, row0=0, rows=None):
    d = x.shape[1]
    rows = x.shape[0] if rows is None else rows
    rb0 = row0 // tm
    return pl.pallas_call(
        _rmsnorm_kernel,
        grid=(rows // tm,),
        in_specs=[pl.BlockSpec((tm, d), lambda i: (rb0 + i, 0)),
                  pl.BlockSpec((1, d), lambda i: (0, 0))],
        out_specs=pl.BlockSpec((tm, d), lambda i: (i, 0)),
        out_shape=jax.ShapeDtypeStruct((rows, d), out_dtype),
        compiler_params=_cparams(("parallel",)),
        name="rmsnorm",
    )(x, w.reshape(1, d).astype(F32))


def _rmsnorm_stack_kernel(x_ref, w_ref, *rest):
    xo_ref, ho_ref = rest[-2:]
    x = x_ref[...]
    xo_ref[...] = x
    y = x * lax.rsqrt(jnp.mean(x * x, axis=-1, keepdims=True) + EPS)
    ho_ref[...] = (y * w_ref[...]).astype(ho_ref.dtype)


def _rmsnorm_stack(parts, w, tm=512):
    d = parts[0].shape[1]
    m = sum(p.shape[0] for p in parts)
    out_shape = [jax.ShapeDtypeStruct((m, d), F32), jax.ShapeDtypeStruct((m, d), BF16)]
    w2 = w.reshape(1, d).astype(F32)
    outs = None
    row0 = 0
    for part in parts:
        rb0 = row0 // tm
        out_spec = pl.BlockSpec((tm, d), lambda i, rb0=rb0: (rb0 + i, 0))
        in_specs = [pl.BlockSpec((tm, d), lambda i: (i, 0)), pl.BlockSpec((1, d), lambda i: (0, 0))]
        args = [part, w2]
        aliases = {}
        if outs is not None:
            in_specs += [pl.BlockSpec(memory_space=pl.ANY)] * 2
            args += list(outs)
            aliases = {2: 0, 3: 1}
        outs = pl.pallas_call(
            _rmsnorm_stack_kernel,
            grid=(part.shape[0] // tm,),
            in_specs=in_specs,
            out_specs=[out_spec, out_spec],
            out_shape=out_shape,
            input_output_aliases=aliases,
            compiler_params=_cparams(("parallel",)),
            name="rmsnorm_stack",
        )(*args)
        row0 += part.shape[0]
    return outs


def _mm_kernel(x_ref, w_ref, o_ref):
    w = w_ref[...].astype(BF16)
    o_ref[...] = jnp.dot(x_ref[...], w, preferred_element_type=F32).astype(o_ref.dtype)


def _mm_residual_kernel(x_ref, w_ref, r_ref, o_ref):
    w = w_ref[...].astype(BF16)
    o_ref[...] = r_ref[...] + jnp.dot(x_ref[...], w, preferred_element_type=F32)


def _gate_up_kernel(x_ref, wg_ref, wu_ref, o_ref):
    x = x_ref[...]
    a = jnp.dot(x, wg_ref[...].astype(BF16), preferred_element_type=F32)
    b = jnp.dot(x, wu_ref[...].astype(BF16), preferred_element_type=F32)
    o_ref[...] = (jax.nn.silu(a) * b).astype(o_ref.dtype)


def _mm_residual_prenorm_kernel(x_ref, w_ref, r_ref, g_ref, o_ref, yg_ref, ssq_ref):
    w = w_ref[...].astype(BF16)
    y = r_ref[...] + jnp.dot(x_ref[...], w, preferred_element_type=F32)
    o_ref[...] = y
    yg_ref[...] = (y * g_ref[...]).astype(yg_ref.dtype)
    sq = y * y
    part = sq[:, :LANES]
    for c in range(1, sq.shape[1] // LANES):
        part = part + sq[:, c * LANES:(c + 1) * LANES]

    @pl.when(pl.program_id(1) == 0)
    def _():
        ssq_ref[...] = part

    @pl.when(pl.program_id(1) > 0)
    def _():
        ssq_ref[...] += part


def _rms_scale_kernel(d_model, ssq_ref, o_ref):
    ms = jnp.sum(ssq_ref[...], axis=-1, keepdims=True) / d_model
    o_ref[...] = jnp.broadcast_to(lax.rsqrt(ms + EPS), o_ref.shape)


def _rms_scale(ssq, d_model, tm=1024):
    m = ssq.shape[0]
    spec = pl.BlockSpec((tm, LANES), lambda i: (i, 0))
    return pl.pallas_call(
        functools.partial(_rms_scale_kernel, d_model),
        grid=(m // tm,),
        in_specs=[spec],
        out_specs=spec,
        out_shape=jax.ShapeDtypeStruct((m, LANES), F32),
        compiler_params=_cparams(("parallel",)),
        name="rms_scale",
    )(ssq)


def _lane_tiled(scale_ref, width):
    r = scale_ref[...]
    return jnp.concatenate([r] * (width // LANES), axis=1)


def _mm_postnorm_kernel(x_ref, w_ref, scale_ref, o_ref):
    r = _lane_tiled(scale_ref, o_ref.shape[1])
    w = w_ref[...].astype(BF16)
    acc = jnp.dot(x_ref[...], w, preferred_element_type=F32)
    o_ref[...] = (r * acc).astype(o_ref.dtype)


def _gate_up_postnorm_kernel(x_ref, wg_ref, wu_ref, scale_ref, o_ref):
    r = _lane_tiled(scale_ref, o_ref.shape[1])
    x = x_ref[...]
    a = r * jnp.dot(x, wg_ref[...].astype(BF16), preferred_element_type=F32)
    b = r * jnp.dot(x, wu_ref[...].astype(BF16), preferred_element_type=F32)
    o_ref[...] = (jax.nn.silu(a) * b).astype(o_ref.dtype)


def _weight_spec(w, layer, tn, col_block0=0):
    return pl.BlockSpec((None, w.shape[1], tn), lambda i, j: (layer, 0, col_block0 + j))


def _matmul(x, w, layer, tm, tn, n_out=None, col_block0=0, scale=None, out_dtype=F32, name="matmul"):
    m, k = x.shape
    n = w.shape[2] if n_out is None else n_out
    in_specs = [pl.BlockSpec((tm, k), lambda i, j: (i, 0)),
                _weight_spec(w, layer, tn, col_block0)]
    args = [x, w]
    body = _mm_kernel
    if scale is not None:
        in_specs.append(pl.BlockSpec((tm, LANES), lambda i, j: (i, 0)))
        args.append(scale)
        body = _mm_postnorm_kernel
    return pl.pallas_call(
        body,
        grid=(m // tm, n // tn),
        in_specs=in_specs,
        out_specs=pl.BlockSpec((tm, tn), lambda i, j: (i, j)),
        out_shape=jax.ShapeDtypeStruct((m, n), out_dtype),
        compiler_params=_cparams(("parallel", "arbitrary")),
        name=name,
    )(*args)


def _matmul_residual(x, w, layer, res, tm, tn, single_buffer_lhs=False, next_norm_w=None,
                     name="matmul_residual"):
    m, k = x.shape
    n = w.shape[2]
    lhs_kwargs = {"pipeline_mode": pl.Buffered(1)} if single_buffer_lhs else {}
    tile = pl.BlockSpec((tm, tn), lambda i, j: (i, j))
    in_specs = [pl.BlockSpec((tm, k), lambda i, j: (i, 0), **lhs_kwargs),
                _weight_spec(w, layer, tn), tile]
    args = [x, w, res]
    if next_norm_w is None:
        body, out_specs, out_shape = _mm_residual_kernel, tile, jax.ShapeDtypeStruct((m, n), F32)
    else:
        in_specs.append(pl.BlockSpec((1, tn), lambda i, j: (0, j)))
        args.append(next_norm_w.reshape(1, n).astype(F32))
        body = _mm_residual_prenorm_kernel
        out_specs = [tile, tile, pl.BlockSpec((tm, LANES), lambda i, j: (i, 0))]
        out_shape = [jax.ShapeDtypeStruct((m, n), F32), jax.ShapeDtypeStruct((m, n), BF16),
                     jax.ShapeDtypeStruct((m, LANES), F32)]
    return pl.pallas_call(
        body,
        grid=(m // tm, n // tn),
        in_specs=in_specs,
        out_specs=out_specs,
        out_shape=out_shape,
        compiler_params=_cparams(("parallel", "arbitrary")),
        name=name,
    )(*args)


def _gate_up(x, wg, wu, layer, tm, tn, scale):
    m, k = x.shape
    n = wg.shape[2]
    return pl.pallas_call(
        _gate_up_postnorm_kernel,
        grid=(m // tm, n // tn),
        in_specs=[pl.BlockSpec((tm, k), lambda i, j: (i, 0)),
                  _weight_spec(wg, layer, tn),
                  _weight_spec(wu, layer, tn),
                  pl.BlockSpec((tm, LANES), lambda i, j: (i, 0))],
        out_specs=pl.BlockSpec((tm, tn), lambda i, j: (i, j)),
        out_shape=jax.ShapeDtypeStruct((m, n), BF16),
        compiler_params=_cparams(("parallel", "arbitrary")),
        name="gate_up",
    )(x, wg, wu, scale)


def _dot_nt(a, b):
    return lax.dot_general(a, b, (((1,), (1,)), ((), ())), preferred_element_type=F32)


def _dot_tn(a, b):
    return lax.dot_general(a, b, (((0,), (0,)), ((), ())), preferred_element_type=F32)


def _dot(a, b):
    return jnp.dot(a, b, preferred_element_type=F32)


def _head_norm(x):
    return x * lax.rsqrt(jnp.mean(x * x, axis=-1, keepdims=True) + EPS)


def _col_to_row(col, eye):
    return jnp.sum(jnp.where(eye, col, 0.0), axis=0, keepdims=True)


def _row_to_col(row, eye):
    return jnp.sum(jnp.where(eye, row, 0.0), axis=1, keepdims=True)


def _alias_existing(mixed, in_specs, args):
    if isinstance(mixed, jax.ShapeDtypeStruct):
        return {}
    in_specs.append(pl.BlockSpec(memory_space=pl.ANY))
    args.append(mixed)
    return {len(args) - 1: 0}


def _rotary(x, cos, sin):
    half = x.shape[-1] // 2
    x1, x2 = x[:, :half], x[:, half:]
    return jnp.concatenate([x1 * cos - x2 * sin, x1 * sin + x2 * cos], axis=-1)


def _ret_kernel(L, units, carry, unroll, *refs):
    (q_ref, k_ref, v_ref, g_ref, cos_ref, sin_ref, dec_ref, qd_ref, kd_ref, cd_ref) = refs[:10]
    s0_ref = None if carry else refs[10]
    o_ref, sout_ref = refs[-2:]

    if carry:
        @pl.when(pl.program_id(2) == 0)
        def _():
            sout_ref[...] = jnp.zeros_like(sout_ref)

    dec = dec_ref[...]
    qd = qd_ref[...]
    kd = kd_ref[...]
    cd = cd_ref[...]
    scale = HEAD_DIM_AB ** -0.5

    def unit(u, c):
        rows = pl.ds(pl.multiple_of(u * L, L), L)
        if carry:
            cos, sin = cos_ref[rows, :], sin_ref[rows, :]
            s = sout_ref[...]
        else:
            cos, sin = cos_ref[...], sin_ref[...]
            s = s0_ref[u]
        q = _rotary(q_ref[rows, :], cos, sin)
        k = _rotary(k_ref[rows, :], cos, sin) * scale
        v = v_ref[rows, :]
        att = _dot_nt(q, k) * dec
        o = _dot(att, v) + _dot(q, s) * qd
        s_new = cd * s + _dot_tn(k * kd, v)
        out = jax.nn.silu(g_ref[rows, :]) * _head_norm(o)
        o_ref[rows, :] = out.astype(o_ref.dtype)
        if carry:
            sout_ref[...] = s_new
        else:
            sout_ref[u] = s_new
        return c

    lax.fori_loop(0, units, unit, 0, unroll=unroll)


def _retention_tables(L):
    log_gamma = jnp.log1p(-jnp.exp2(-5.0 - jnp.arange(RET_HEADS, dtype=F32)))[:, None]
    idx = jnp.arange(L, dtype=F32)
    diff = idx[:, None] - idx[None, :]
    decay = jnp.exp(jnp.where(diff >= 0, log_gamma[:, :, None] * diff, -jnp.inf))
    q_decay = jnp.exp(log_gamma * (idx + 1.0))[:, :, None]
    k_decay = jnp.exp(log_gamma * (L - 1.0 - idx))[:, :, None]
    chunk_decay = jnp.exp(log_gamma[:, 0] * L)[:, None, None]
    return decay, q_decay, k_decay, chunk_decay


def _rope_tables(pos):
    half = HEAD_DIM_AB // 2
    inv_freq = ROPE_BASE ** (-jnp.arange(half, dtype=F32) / half)
    ang = pos[:, None] * inv_freq[None, :]
    return jnp.cos(ang), jnp.sin(ang)


def _retention(proj, mixed, *, row0, batch, seq, L, rows_per_step, pos0, s0=None, unroll=1):
    d = HEAD_DIM_AB
    h_ = RET_HEADS
    carry = s0 is None
    units = rows_per_step // L
    decay, q_decay, k_decay, chunk_decay = _retention_tables(L)
    cos, sin = _rope_tables(pos0 + jnp.arange(seq, dtype=F32))
    rb0 = row0 // rows_per_step
    if carry:
        tblocks = seq // rows_per_step
        grid = (batch, h_, tblocks)
        row_idx = lambda b, h, t: rb0 + b * tblocks + t
        rope_spec = pl.BlockSpec((rows_per_step, d // 2), lambda b, h, t: (t, 0))
        sout_spec = pl.BlockSpec((None, None, d, d), lambda b, h, t: (b, h, 0, 0))
    else:
        nb = rows_per_step // seq
        grid = (batch // nb, h_, 1)
        row_idx = lambda b, h, t: rb0 + b
        rope_spec = pl.BlockSpec((seq, d // 2), lambda b, h, t: (0, 0))
        sout_spec = pl.BlockSpec((nb, None, d, d), lambda b, h, t: (b, h, 0, 0))

    def col_spec(c0):
        return pl.BlockSpec((rows_per_step, d), lambda b, h, t: (row_idx(b, h, t), c0 + h))

    in_specs = [col_spec(0), col_spec(h_), col_spec(2 * h_), col_spec(3 * h_),
                rope_spec, rope_spec,
                pl.BlockSpec((None, L, L), lambda b, h, t: (h, 0, 0)),
                pl.BlockSpec((None, L, 1), lambda b, h, t: (h, 0, 0)),
                pl.BlockSpec((None, L, 1), lambda b, h, t: (h, 0, 0)),
                pl.BlockSpec((None, 1, 1), lambda b, h, t: (h, 0, 0))]
    args = [proj, proj, proj, proj, cos, sin, decay, q_decay, k_decay, chunk_decay]
    if not carry:
        in_specs.append(sout_spec)
        args.append(s0)
    aliases = _alias_existing(mixed, in_specs, args)
    return pl.pallas_call(
        functools.partial(_ret_kernel, L, units, carry, unroll),
        grid=grid,
        in_specs=in_specs,
        out_specs=[col_spec(0), sout_spec],
        out_shape=[jax.ShapeDtypeStruct(mixed.shape, mixed.dtype),
                   jax.ShapeDtypeStruct((batch, h_, d, d), F32)],
        input_output_aliases=aliases,
        compiler_params=_recurrence_cparams(),
        name="retention_prompt" if carry else "retention_sample",
    )(*args)


def _mlstm_kernel(L, units, carry, unroll, *refs):
    (q_ref, k_ref, v_ref, og_ref, gates_ref, bias_ref, nw_ref) = refs[:7]
    c0_ref, n0_ref, m0_ref = (None, None, None) if carry else refs[7:10]
    o_ref, c_ref, n_ref, m_ref = refs[-4:]

    if carry:
        @pl.when(pl.program_id(2) == 0)
        def _():
            c_ref[...] = jnp.zeros_like(c_ref)
            n_ref[...] = jnp.zeros_like(n_ref)
            m_ref[...] = jnp.zeros_like(m_ref)

    head = pl.program_id(1)
    bias = bias_ref[...]
    nw = nw_ref[...]
    lane = lax.broadcasted_iota(jnp.int32, (L, GATE_LANES), 1)
    ti = lax.broadcasted_iota(jnp.int32, (L, L), 0)
    si = lax.broadcasted_iota(jnp.int32, (L, L), 1)
    eye = ti == si
    causal = si <= ti
    scale = HEAD_DIM_AB ** -0.5

    def unit(u, carry_val):
        rows = pl.ds(pl.multiple_of(u * L, L), L)
        if carry:
            c, n, m = c_ref[...], n_ref[...], m_ref[:, 0:1]
        else:
            c, n, m = c0_ref[u], n0_ref[u], m0_ref[u][:, 0:1]
        x = gates_ref[rows, :] + bias
        ig = jnp.sum(jnp.where(lane == head, x, 0.0), axis=1, keepdims=True)
        fpre = jnp.sum(jnp.where(lane == head + ML_HEADS, x, 0.0), axis=1, keepdims=True)
        lf = jnp.minimum(fpre, 0.0) - jnp.log1p(jnp.exp(-jnp.abs(fpre)))
        f_cum = jnp.sum(jnp.where(causal, _col_to_row(lf, eye), 0.0), axis=1, keepdims=True)
        a = ig - f_cum
        a_row = _col_to_row(a, eye)
        cmax = jnp.max(jnp.where(causal, a_row, -jnp.inf), axis=1, keepdims=True)
        m_t = f_cum + jnp.maximum(m, cmax)
        d_mat = jnp.exp(jnp.where(causal, (f_cum - m_t) + a_row, -jnp.inf))
        q = q_ref[rows, :]
        k = k_ref[rows, :] * scale
        v = v_ref[rows, :]
        s = _dot_nt(q, k) * d_mat
        inter = jnp.exp(f_cum + m - m_t)
        num = inter * _dot(q, c) + _dot(s, v)
        den = inter * jnp.sum(q * n, axis=1, keepdims=True) + jnp.sum(s, axis=1, keepdims=True)
        hh = num * (1.0 / jnp.maximum(jnp.abs(den), jnp.exp(-m_t)))
        f_end = f_cum[L - 1:L, :]
        m_end = m_t[L - 1:L, :]
        w = jnp.exp(f_end - m_end + a)
        cdec = jnp.exp(f_end + m - m_end)
        kw = k * w
        c_new = cdec * c + _dot_tn(kw, v)
        n_new = cdec * n + jnp.sum(kw, axis=0, keepdims=True)
        out = jax.nn.sigmoid(og_ref[rows, :]) * (_head_norm(hh) * nw)
        o_ref[rows, :] = out.astype(o_ref.dtype)
        m_new = jnp.broadcast_to(m_end, (1, GATE_LANES))
        if carry:
            c_ref[...] = c_new
            n_ref[...] = n_new
            m_ref[...] = m_new
        else:
            c_ref[u] = c_new
            n_ref[u] = n_new
            m_ref[u] = m_new
        return carry_val

    lax.fori_loop(0, units, unit, 0, unroll=unroll)


def _mlstm(proj, gates, bias, norm_w, mixed, *, row0, batch, seq, L, rows_per_step,
           c0=None, n0=None, m0=None, unroll=1):
    d = HEAD_DIM_AB
    h_ = ML_HEADS
    carry = c0 is None
    units = rows_per_step // L
    rb0 = row0 // rows_per_step
    col0 = 4 * RET_HEADS
    if carry:
        tblocks = seq // rows_per_step
        grid = (batch, h_, tblocks)
        row_idx = lambda b, h, t: rb0 + b * tblocks + t
        lead = None
    else:
        nb = rows_per_step // seq
        grid = (batch // nb, h_, 1)
        row_idx = lambda b, h, t: rb0 + b
        lead = nb
    c_spec = pl.BlockSpec((lead, None, d, d), lambda b, h, t: (b, h, 0, 0))
    n_spec = pl.BlockSpec((lead, None, 1, d), lambda b, h, t: (b, h, 0, 0))
    m_spec = pl.BlockSpec((lead, None, 1, GATE_LANES), lambda b, h, t: (b, h, 0, 0))

    def col_spec(c_blk):
        return pl.BlockSpec((rows_per_step, d), lambda b, h, t: (row_idx(b, h, t), c_blk + h))

    in_specs = [col_spec(col0), col_spec(col0 + h_), col_spec(col0 + 2 * h_), col_spec(col0 + 3 * h_),
                pl.BlockSpec((rows_per_step, GATE_LANES), lambda b, h, t: (row_idx(b, h, t), 0)),
                pl.BlockSpec((1, GATE_LANES), lambda b, h, t: (0, 0)),
                pl.BlockSpec((1, d), lambda b, h, t: (0, h))]
    args = [proj, proj, proj, proj, gates, bias, norm_w]
    if not carry:
        in_specs += [c_spec, n_spec, m_spec]
        args += [c0, n0, m0]
    aliases = _alias_existing(mixed, in_specs, args)
    return pl.pallas_call(
        functools.partial(_mlstm_kernel, L, units, carry, unroll),
        grid=grid,
        in_specs=in_specs,
        out_specs=[col_spec(RET_HEADS), c_spec, n_spec, m_spec],
        out_shape=[jax.ShapeDtypeStruct(mixed.shape, mixed.dtype),
                   jax.ShapeDtypeStruct((batch, h_, d, d), F32),
                   jax.ShapeDtypeStruct((batch, h_, 1, d), F32),
                   jax.ShapeDtypeStruct((batch, h_, 1, GATE_LANES), F32)],
        input_output_aliases=aliases,
        compiler_params=_recurrence_cparams(),
        name="mlstm_prompt" if carry else "mlstm_sample",
    )(*args)


def _hgrn_kernel(L, units, carry, unroll, *refs):
    (q_ref, f_ref, i_ref, g_ref, lbl_ref, nw_ref) = refs[:6]
    rows_ref = refs[-1]
    if carry:
        s0_ref = None
        o_ref, sout_ref, st_ref = refs[-4:-1]
    else:
        s0_ref = refs[6]
        o_ref, sout_ref, kd_ref, eg_ref = refs[-5:-1]
        st_ref = None

    if carry:
        @pl.when(pl.program_id(2) == 0)
        def _():
            st_ref[...] = jnp.zeros_like(st_ref)

    d = HG_DIM
    lg = lbl_ref[...]
    e = jnp.exp(lg - jnp.max(lg, axis=0, keepdims=True))
    sm = e / jnp.sum(e, axis=0, keepdims=True)
    lb = (sm[0:1, :] + sm[1:2, :]) - sm[0:1, :]
    nw = nw_ref[...]
    groups = L // 8
    pairs = [(j, jp) for jp in range(groups - 1) for j in range(jp + 1, groups)]
    row8 = lax.broadcasted_iota(jnp.int32, (8, d), 0)
    lane = lax.broadcasted_iota(jnp.int32, (8, L), 1)
    if not carry:
        eye_d = (lax.broadcasted_iota(jnp.int32, (d, d), 0)
                 == lax.broadcasted_iota(jnp.int32, (d, d), 1))

    def unit(u, carry_val):
        rows = pl.ds(pl.multiple_of(u * L, L), L)
        q = jax.nn.silu(q_ref[rows, :])
        fg = lb + (1.0 - lb) * jax.nn.sigmoid(f_ref[rows, :])
        k = 1.0 - fg
        logf = jnp.log(fg)
        v = i_ref[rows, :]
        gg = []
        base = jnp.zeros((1, d), F32)
        for j in range(groups):
            cum = logf[8 * j:8 * j + 8, :]
            for sh in (1, 2, 4):
                cum = cum + jnp.where(row8 >= sh, pltpu.roll(cum, sh, 0), 0.0)
            cum = cum + base
            gg.append(cum)
            base = cum[7:8, :]
        g_end = base
        g_cum = jnp.concatenate(gg, axis=0)
        qg = [q[8 * j:8 * j + 8, :] for j in range(groups)]
        kg = [k[8 * j:8 * j + 8, :] for j in range(groups)]
        rows_ref[0, rows, :] = g_cum
        rows_ref[1, rows, :] = k
        row0 = pl.multiple_of(u * L, L)
        att = []
        for j in range(groups):
            a = jnp.zeros((8, L), F32)
            for r in range(8):
                g_r = rows_ref[0, pl.ds(row0 + 8 * j + r, 1), :]
                k_r = rows_ref[1, pl.ds(row0 + 8 * j + r, 1), :]
                diff = jnp.where(row8 >= r, gg[j] - g_r, -jnp.inf)
                col = jnp.sum(qg[j] * k_r * jnp.exp(diff), axis=1, keepdims=True)
                a = jnp.where(lane == 8 * j + r, col, a)
            att.append(a)
        if pairs:
            ends = [gg[j][7:8, :] for j in range(groups)]
            kt = jnp.concatenate([kg[j] * jnp.exp(ends[j] - gg[j]) for j in range(groups)], axis=0)
            qt = jnp.concatenate([qg[j] * jnp.exp(gg[j] - ends[jp]) for j, jp in pairs], axis=0)
            cross = _dot_nt(qt, kt)
            for idx, (j, jp) in enumerate(pairs):
                in_group = (lane >= 8 * jp) & (lane < 8 * jp + 8)
                att[j] = jnp.where(in_group, cross[8 * idx:8 * idx + 8, :], att[j])
        att_m = jnp.concatenate(att, axis=0)
        qs = q * jnp.exp(g_cum)
        kd = k * jnp.exp(g_end - g_cum)
        if carry:
            st = st_ref[...]
            o = _dot_nt(qs, st) + _dot(att_m, v)
            st_ref[...] = st * jnp.exp(g_end) + _dot_tn(v, kd)
        else:
            o = _dot(qs, s0_ref[u]) + _dot(att_m, v)
            kd_ref[rows, :] = kd
            eg_ref[pl.ds(u, 1), :] = jnp.exp(g_end)
        out = (_head_norm(o) * nw) * jax.nn.silu(g_ref[rows, :])
        o_ref[rows, :] = out.astype(o_ref.dtype)
        return carry_val

    lax.fori_loop(0, units, unit, 0, unroll=unroll)

    if carry:
        @pl.when(pl.program_id(2) == pl.num_programs(2) - 1)
        def _():
            sout_ref[...] = st_ref[...].T
    else:
        per_tile = d // L
        row_chunk = lax.broadcasted_iota(jnp.int32, (d, d), 0) // L
        for t in range(units // per_tile):
            kd_t = kd_ref[t * d:(t + 1) * d, :].T
            v_t = i_ref[t * d:(t + 1) * d, :]
            for c in range(per_tile):
                u = t * per_tile + c
                upd = _dot(kd_t, jnp.where(row_chunk == c, v_t, 0.0))
                sout_ref[u] = _row_to_col(eg_ref[u:u + 1, :], eye_d) * s0_ref[u] + upd


def _hgrn(proj, lb_logits, norm_w, mixed, *, row0, batch, seq, L, rows_per_step, s0=None, unroll=1):
    d = HG_DIM
    h_ = proj.shape[1] // (4 * d)
    carry = s0 is None
    units = rows_per_step // L
    rb0 = row0 // rows_per_step
    if carry:
        tblocks = seq // rows_per_step
        grid = (batch, h_, tblocks)
        row_idx = lambda b, h, t: rb0 + b * tblocks + t
        sout_spec = pl.BlockSpec((None, None, d, d), lambda b, h, t: (b, h, 0, 0))
        scratch = [pltpu.VMEM((d, d), F32), pltpu.VMEM((2, rows_per_step, d), F32)]
    else:
        nb = rows_per_step // seq
        grid = (batch // nb, h_, 1)
        row_idx = lambda b, h, t: rb0 + b
        sout_spec = pl.BlockSpec((nb, None, d, d), lambda b, h, t: (b, h, 0, 0))
        scratch = [pltpu.VMEM((rows_per_step, d), F32), pltpu.VMEM((units, d), F32),
                   pltpu.VMEM((2, rows_per_step, d), F32)]

    def col_spec(c0):
        return pl.BlockSpec((rows_per_step, d), lambda b, h, t: (row_idx(b, h, t), c0 + h))

    in_specs = [col_spec(0), col_spec(h_), col_spec(2 * h_), col_spec(3 * h_),
                pl.BlockSpec((2, d), lambda b, h, t: (0, h)),
                pl.BlockSpec((1, d), lambda b, h, t: (0, 0))]
    args = [proj, proj, proj, proj, lb_logits, norm_w]
    if not carry:
        in_specs.append(sout_spec)
        args.append(s0)
    aliases = _alias_existing(mixed, in_specs, args)
    return pl.pallas_call(
        functools.partial(_hgrn_kernel, L, units, carry, unroll),
        grid=grid,
        in_specs=in_specs,
        out_specs=[col_spec(0), sout_spec],
        out_shape=[jax.ShapeDtypeStruct(mixed.shape, mixed.dtype),
                   jax.ShapeDtypeStruct((batch, h_, d, d), F32)],
        scratch_shapes=scratch,
        input_output_aliases=aliases,
        compiler_params=_recurrence_cparams(),
        name="hgrn_prompt" if carry else "hgrn_sample",
    )(*args)


def _chunk_len(t, c):
    return c if t % c == 0 else t


def _swiglu_residual(x, xg, ssq, w_gate, w_up, w_down, layer, next_norm_w):
    act = _gate_up(xg, w_gate, w_up, layer, TM_GATE_UP, 256, _rms_scale(ssq, x.shape[1]))
    return _matmul_residual(act, w_down, layer, x, TM_DOWN, 256, single_buffer_lhs=True,
                            next_norm_w=next_norm_w, name="down_proj")


def kernel(x_prompt, x_sample, state_ret, state_mlstm_C, state_mlstm_n, state_mlstm_m, state_hgrn,
           norm_mix_w, w_in_ab, b_if_ab, ml_norm_w, w_out_ab, w_in_c, lb_logits, hg_norm_w, w_out_c,
           norm_ffn_w, w_gate, w_up, w_down, norm_final_w):
    bp, tp, d = x_prompt.shape
    bs, ts, _ = x_sample.shape
    mp, ms = bp * tp, bs * ts
    tm = TM_WIDE

    w_in_ab_b, w_out_ab_b = w_in_ab.astype(BF16), w_out_ab
    w_in_c_b, w_out_c_b = w_in_c, w_out_c
    w_gate_b, w_up_b, w_down_b = w_gate, w_up, w_down

    ab_main = 4 * RET_HEADS * HEAD_DIM_AB + 4 * ML_HEADS * HEAD_DIM_AB
    x, hn = _rmsnorm_stack([x_prompt.reshape(mp, d), x_sample.reshape(ms, d)], norm_mix_w[0])
    proj = _matmul(hn, w_in_ab_b, 0, tm, 512, n_out=ab_main, name="in_proj_ab")
    gates = _matmul(hn, w_in_ab_b, 0, tm, GATE_LANES, n_out=GATE_LANES,
                    col_block0=ab_main // GATE_LANES, name="in_proj_gates")
    bias = jnp.pad(b_if_ab[0].astype(F32), (0, GATE_LANES - 2 * ML_HEADS)).reshape(1, GATE_LANES)
    ml_w = ml_norm_w[0].astype(F32).reshape(1, -1)

    mixed = jax.ShapeDtypeStruct((mp + ms, d), BF16)
    lp = _chunk_len(tp, RET_CHUNK)
    ls = _chunk_len(ts, RET_CHUNK)
    mixed, ret_p = _retention(proj, mixed, row0=0, batch=bp, seq=tp, L=lp, rows_per_step=4 * lp,
                              pos0=0.0, unroll=UNROLL_AB_PROMPT)
    mixed, ret_s = _retention(proj, mixed, row0=mp, batch=bs, seq=ts, L=ls, rows_per_step=8 * ls,
                              pos0=float(PAST_LEN), s0=state_ret[0], unroll=UNROLL_AB_SAMPLE)
    lp = _chunk_len(tp, ML_CHUNK)
    ls = _chunk_len(ts, ML_CHUNK)
    mixed, mc_p, mn_p, mm_p = _mlstm(proj, gates, bias, ml_w, mixed, row0=0, batch=bp, seq=tp, L=lp,
                                     rows_per_step=4 * lp, unroll=UNROLL_AB_PROMPT)
    m0 = jnp.broadcast_to(state_mlstm_m[0][:, :, None, None], (bs, ML_HEADS, 1, GATE_LANES))
    mixed, mc_s, mn_s, mm_s = _mlstm(proj, gates, bias, ml_w, mixed, row0=mp, batch=bs, seq=ts, L=ls,
                                     rows_per_step=8 * ls, c0=state_mlstm_C[0],
                                     n0=state_mlstm_n[0][:, :, None, :], m0=m0, unroll=UNROLL_AB_SAMPLE)
    x, xg, ssq = _matmul_residual(mixed, w_out_ab_b, 0, x, TM_OUT, 512, next_norm_w=norm_ffn_w[0],
                                  name="out_proj_ab")
    x, xg, ssq = _swiglu_residual(x, xg, ssq, w_gate_b, w_up_b, w_down_b, 0, norm_mix_w[1])

    proj = _matmul(xg, w_in_c_b, 0, tm, 512, scale=_rms_scale(ssq, d), name="in_proj_c")
    mixed = jax.ShapeDtypeStruct((mp + ms, d), BF16)
    hg_w =hg_norm_w[0].astype(F32).reshape(1, -1)
    lbl = lb_logits.astype(F32)
    lp = _chunk_len(tp, HG_CHUNK)
    ls = _chunk_len(ts, HG_CHUNK)
    mixed, hg_p = _hgrn(proj, lbl, hg_w, mixed, row0=0, batch=bp, seq=tp, L=lp, rows_per_step=32 * lp,
                        unroll=UNROLL_HG_PROMPT)
    mixed, hg_s = _hgrn(proj, lbl, hg_w, mixed, row0=mp, batch=bs, seq=ts, L=ls, rows_per_step=32 * ls,
                        s0=state_hgrn[0], unroll=UNROLL_HG_SAMPLE)
    x, xg, ssq = _matmul_residual(mixed, w_out_c_b, 0, x, TM_OUT, 512, next_norm_w=norm_ffn_w[1],
                                  name="out_proj_c")
    x = _swiglu_residual(x, xg, ssq, w_gate_b, w_up_b, w_down_b, 1, None)

    y_prompt = _rmsnorm(x, norm_final_w, F32, row0=0, rows=mp).reshape(bp, tp, d)
    y_sample = _rmsnorm(x, norm_final_w, F32, row0=mp, rows=ms).reshape(bs, ts, d)
    return (y_prompt, y_sample,
            ret_p[None], mc_p[None], mn_p[:, :, 0, :][None], mm_p[:, :, 0, 0][None], hg_p[None],
            ret_s[None], mc_s[None], mn_s[:, :, 0, :][None], mm_s[:, :, 0, 0][None], hg_s[None])
```

```python
import functools

import numpy as np
import jax
import jax.numpy as jnp
from jax import lax
from jax.experimental import pallas as pl
from jax.experimental.pallas import tpu as pltpu

F32 = jnp.float32
BF16 = jnp.bfloat16

EPS = 1e-6
ROPE_BASE = 10000.0
PAST_LEN = 16384

RET_HEADS = 8
ML_HEADS = 8
HEAD_DIM_AB = 256
HG_DIM = 128
RET_CHUNK = 128
ML_CHUNK = 128
HG_CHUNK = 32
LANES = 128
GATE_LANES = LANES

V7X_VMEM_LIMIT_BYTES = 56 * 1024 * 1024

TM_WIDE = 1536
TM_GATE_UP = TM_WIDE
TM_OUT = 1024
TM_DOWN = 1024

UNROLL_AB_PROMPT = 2
UNROLL_AB_SAMPLE = 8
UNROLL_HG_PROMPT = 32
UNROLL_HG_SAMPLE = 8


def _cparams(semantics, **kwargs):
    return pltpu.CompilerParams(dimension_semantics=semantics,
                                vmem_limit_bytes=V7X_VMEM_LIMIT_BYTES, **kwargs)


def _recurrence_cparams():
    return _cparams(("parallel", "parallel", "arbitrary"))


def _rmsnorm_kernel(x_ref, w_ref, o_ref):
    x = x_ref[...]
    y = x * lax.rsqrt(jnp.mean(x * x, axis=-1, keepdims=True) + EPS)
    o_ref[...] = (y * w_ref[...]).astype(o_ref.dtype)


def _rmsnorm(x, w, out_dtype, tm=512, row0=0, rows=None):
    d = x.shape[1]
    rows = x.shape[0] if rows is None else rows
    rb0 = row0 // tm
    return pl.pallas_call(
        _rmsnorm_kernel,
        grid=(rows // tm,),
        in_specs=[pl.BlockSpec((tm, d), lambda i: (rb0 + i, 0)),
                  pl.BlockSpec((1, d), lambda i: (0, 0))],
        out_specs=pl.BlockSpec((tm, d), lambda i: (i, 0)),
        out_shape=jax.ShapeDtypeStruct((rows, d), out_dtype),
        compiler_params=_cparams(("parallel",)),
        name="rmsnorm",
    )(x, w.reshape(1, d).astype(F32))


def _rmsnorm_stack_kernel(x_ref, w_ref, *rest):
    xo_ref, ho_ref = rest[-2:]
    x = x_ref[...]
    xo_ref[...] = x
    y = x * lax.rsqrt(jnp.mean(x * x, axis=-1, keepdims=True) + EPS)
    ho_ref[...] = (y * w_ref[...]).astype(ho_ref.dtype)


def _rmsnorm_stack(parts, w, tm=512):
    d = parts[0].shape[1]
    m = sum(p.shape[0] for p in parts)
    out_shape = [jax.ShapeDtypeStruct((m, d), F32), jax.ShapeDtypeStruct((m, d), BF16)]
    w2 = w.reshape(1, d).astype(F32)
    outs = None
    row0 = 0
    for part in parts:
        rb0 = row0 // tm
        out_spec = pl.BlockSpec((tm, d), lambda i, rb0=rb0: (rb0 + i, 0))
        in_specs = [pl.BlockSpec((tm, d), lambda i: (i, 0)), pl.BlockSpec((1, d), lambda i: (0, 0))]
        args = [part, w2]
        aliases = {}
        if outs is not None:
            in_specs += [pl.BlockSpec(memory_space=pl.ANY)] * 2
            args += list(outs)
            aliases = {2: 0, 3: 1}
        outs = pl.pallas_call(
            _rmsnorm_stack_kernel,
            grid=(part.shape[0] // tm,),
            in_specs=in_specs,
            out_specs=[out_spec, out_spec],
            out_shape=out_shape,
            input_output_aliases=aliases,
            compiler_params=_cparams(("parallel",)),
            name="rmsnorm_stack",
        )(*args)
        row0 += part.shape[0]
    return outs


def _mm_kernel(x_ref, w_ref, o_ref):
    w = w_ref[...].astype(BF16)
    o_ref[...] = jnp.dot(x_ref[...], w, preferred_element_type=F32).astype(o_ref.dtype)


def _mm_residual_kernel(x_ref, w_ref, r_ref, o_ref):
    w = w_ref[...].astype(BF16)
    o_ref[...] = r_ref[...] + jnp.dot(x_ref[...], w, preferred_element_type=F32)


def _gate_up_kernel(x_ref, wg_ref, wu_ref, o_ref):
    x = x_ref[...]
    a = jnp.dot(x, wg_ref[...].astype(BF16), preferred_element_type=F32)
    b = jnp.dot(x, wu_ref[...].astype(BF16), preferred_element_type=F32)
    o_ref[...] = (jax.nn.silu(a) * b).astype(o_ref.dtype)


def _mm_residual_prenorm_kernel(x_ref, w_ref, r_ref, g_ref, o_ref, yg_ref, ssq_ref):
    w = w_ref[...].astype(BF16)
    y = r_ref[...] + jnp.dot(x_ref[...], w, preferred_element_type=F32)
    o_ref[...] = y
    yg_ref[...] = (y * g_ref[...]).astype(yg_ref.dtype)
    sq = y * y
    part = sq[:, :LANES]
    for c in range(1, sq.shape[1] // LANES):
        part = part + sq[:, c * LANES:(c + 1) * LANES]

    @pl.when(pl.program_id(1) == 0)
    def _():
        ssq_ref[...] = part

    @pl.when(pl.program_id(1) > 0)
    def _():
        ssq_ref[...] += part


def _rms_scale_kernel(d_model, ssq_ref, o_ref):
    ms = jnp.sum(ssq_ref[...], axis=-1, keepdims=True) / d_model
    o_ref[...] = jnp.broadcast_to(lax.rsqrt(ms + EPS), o_ref.shape)


def _rms_scale(ssq, d_model, tm=1024):
    m = ssq.shape[0]
    spec = pl.BlockSpec((tm, LANES), lambda i: (i, 0))
    return pl.pallas_call(
        functools.partial(_rms_scale_kernel, d_model),
        grid=(m // tm,),
        in_specs=[spec],
        out_specs=spec,
        out_shape=jax.ShapeDtypeStruct((m, LANES), F32),
        compiler_params=_cparams(("parallel",)),
        name="rms_scale",
    )(ssq)


def _lane_tiled(scale_ref, width):
    r = scale_ref[...]
    return jnp.concatenate([r] * (width // LANES), axis=1)


def _mm_postnorm_kernel(x_ref, w_ref, scale_ref, o_ref):
    r = _lane_tiled(scale_ref, o_ref.shape[1])
    w = w_ref[...].astype(BF16)
    acc = jnp.dot(x_ref[...], w, preferred_element_type=F32)
    o_ref[...] = (r * acc).astype(o_ref.dtype)


def _gate_up_postnorm_kernel(x_ref, wg_ref, wu_ref, scale_ref, o_ref):
    r = _lane_tiled(scale_ref, o_ref.shape[1])
    x = x_ref[...]
    a = r * jnp.dot(x, wg_ref[...].astype(BF16), preferred_element_type=F32)
    b = r * jnp.dot(x, wu_ref[...].astype(BF16), preferred_element_type=F32)
    o_ref[...] = (jax.nn.silu(a) * b).astype(o_ref.dtype)


def _weight_spec(w, layer, tn, col_block0=0):
    return pl.BlockSpec((None, w.shape[1], tn), lambda i, j: (layer, 0, col_block0 + j))


def _matmul(x, w, layer, tm, tn, n_out=None, col_block0=0, scale=None, out_dtype=F32, name="matmul"):
    m, k = x.shape
    n = w.shape[2] if n_out is None else n_out
    in_specs = [pl.BlockSpec((tm, k), lambda i, j: (i, 0)),
                _weight_spec(w, layer, tn, col_block0)]
    args = [x, w]
    body = _mm_kernel
    if scale is not None:
        in_specs.append(pl.BlockSpec((tm, LANES), lambda i, j: (i, 0)))
        args.append(scale)
        body = _mm_postnorm_kernel
    return pl.pallas_call(
        body,
        grid=(m // tm, n // tn),
        in_specs=in_specs,
        out_specs=pl.BlockSpec((tm, tn), lambda i, j: (i, j)),
        out_shape=jax.ShapeDtypeStruct((m, n), out_dtype),
        compiler_params=_cparams(("parallel", "arbitrary")),
        name=name,
    )(*args)


def _matmul_residual(x, w, layer, res, tm, tn, single_buffer_lhs=False, next_norm_w=None,
                     name="matmul_residual"):
    m, k = x.shape
    n = w.shape[2]
    lhs_kwargs = {"pipeline_mode": pl.Buffered(1)} if single_buffer_lhs else {}
    tile = pl.BlockSpec((tm, tn), lambda i, j: (i, j))
    in_specs = [pl.BlockSpec((tm, k), lambda i, j: (i, 0), **lhs_kwargs),
                _weight_spec(w, layer, tn), tile]
    args = [x, w, res]
    if next_norm_w is None:
        body, out_specs, out_shape = _mm_residual_kernel, tile, jax.ShapeDtypeStruct((m, n), F32)
    else:
        in_specs.append(pl.BlockSpec((1, tn), lambda i, j: (0, j)))
        args.append(next_norm_w.reshape(1, n).astype(F32))
        body = _mm_residual_prenorm_kernel
        out_specs = [tile, tile, pl.BlockSpec((tm, LANES), lambda i, j: (i, 0))]
        out_shape = [jax.ShapeDtypeStruct((m, n), F32), jax.ShapeDtypeStruct((m, n), BF16),
                     jax.ShapeDtypeStruct((m, LANES), F32)]
    return pl.pallas_call(
        body,
        grid=(m // tm, n // tn),
        in_specs=in_specs,
        out_specs=out_specs,
        out_shape=out_shape,
        compiler_params=_cparams(("parallel", "arbitrary")),
        name=name,
    )(*args)


def _gate_up(x, wg, wu, layer, tm, tn, scale):
    m, k = x.shape
    n = wg.shape[2]
    return pl.pallas_call(
        _gate_up_postnorm_kernel,
        grid=(m // tm, n // tn),
        in_specs=[pl.BlockSpec((tm, k), lambda i, j: (i, 0)),
                  _weight_spec(wg, layer, tn),
                  _weight_spec(wu, layer, tn),
                  pl.BlockSpec((tm, LANES), lambda i, j: (i, 0))],
        out_specs=pl.BlockSpec((tm, tn), lambda i, j: (i, j)),
        out_shape=jax.ShapeDtypeStruct((m, n), BF16),
        compiler_params=_cparams(("parallel", "arbitrary")),
        name="gate_up",
    )(x, wg, wu, scale)


def _dot_nt(a, b):
    return lax.dot_general(a, b, (((1,), (1,)), ((), ())), preferred_element_type=F32)


def _dot_tn(a, b):
    return lax.dot_general(a, b, (((0,), (0,)), ((), ())), preferred_element_type=F32)


def _dot(a, b):
    return jnp.dot(a, b, preferred_element_type=F32)


def _head_norm(x):
    return x * lax.rsqrt(jnp.mean(x * x, axis=-1, keepdims=True) + EPS)


def _col_to_row(col, eye):
    return jnp.sum(jnp.where(eye, col, 0.0), axis=0, keepdims=True)


def _row_to_col(row, eye):
    return jnp.sum(jnp.where(eye, row, 0.0), axis=1, keepdims=True)


def _alias_existing(mixed, in_specs, args):
    if isinstance(mixed, jax.ShapeDtypeStruct):
        return {}
    in_specs.append(pl.BlockSpec(memory_space=pl.ANY))
    args.append(mixed)
    return {len(args) - 1: 0}


def _rotary(x, cos, sin):
    half = x.shape[-1] // 2
    x1, x2 = x[:, :half], x[:, half:]
    return jnp.concatenate([x1 * cos - x2 * sin, x1 * sin + x2 * cos], axis=-1)


def _ret_kernel(L, units, carry, unroll, *refs):
    (q_ref, k_ref, v_ref, g_ref, cos_ref, sin_ref, dec_ref, qd_ref, kd_ref, cd_ref) = refs[:10]
    s0_ref = None if carry else refs[10]
    o_ref, sout_ref = refs[-2:]

    if carry:
        @pl.when(pl.program_id(2) == 0)
        def _():
            sout_ref[...] = jnp.zeros_like(sout_ref)

    dec = dec_ref[...]
    qd = qd_ref[...]
    kd = kd_ref[...]
    cd = cd_ref[...]
    scale = HEAD_DIM_AB ** -0.5

    def unit(u, c):
        rows = pl.ds(pl.multiple_of(u * L, L), L)
        if carry:
            cos, sin = cos_ref[rows, :], sin_ref[rows, :]
            s = sout_ref[...]
        else:
            cos, sin = cos_ref[...], sin_ref[...]
            s = s0_ref[u]
        q = _rotary(q_ref[rows, :], cos, sin)
        k = _rotary(k_ref[rows, :], cos, sin) * scale
        v = v_ref[rows, :]
        att = _dot_nt(q, k) * dec
        o = _dot(att, v) + _dot(q, s) * qd
        s_new = cd * s + _dot_tn(k * kd, v)
        out = jax.nn.silu(g_ref[rows, :]) * _head_norm(o)
        o_ref[rows, :] = out.astype(o_ref.dtype)
        if carry:
            sout_ref[...] = s_new
        else:
            sout_ref[u] = s_new
        return c

    lax.fori_loop(0, units, unit, 0, unroll=unroll)


def _retention_tables(L):
    log_gamma = jnp.log1p(-jnp.exp2(-5.0 - jnp.arange(RET_HEADS, dtype=F32)))[:, None]
    idx = jnp.arange(L, dtype=F32)
    diff = idx[:, None] - idx[None, :]
    decay = jnp.exp(jnp.where(diff >= 0, log_gamma[:, :, None] * diff, -jnp.inf))
    q_decay = jnp.exp(log_gamma * (idx + 1.0))[:, :, None]
    k_decay = jnp.exp(log_gamma * (L - 1.0 - idx))[:, :, None]
    chunk_decay = jnp.exp(log_gamma[:, 0] * L)[:, None, None]
    return decay, q_decay, k_decay, chunk_decay


def _rope_tables(pos):
    half = HEAD_DIM_AB // 2
    inv_freq = ROPE_BASE ** (-jnp.arange(half, dtype=F32) / half)
    ang = pos[:, None] * inv_freq[None, :]
    return jnp.cos(ang), jnp.sin(ang)


def _retention(proj, mixed, *, row0, batch, seq, L, rows_per_step, pos0, s0=None, unroll=1):
    d = HEAD_DIM_AB
    h_ = RET_HEADS
    carry = s0 is None
    units = rows_per_step // L
    decay, q_decay, k_decay, chunk_decay = _retention_tables(L)
    cos, sin = _rope_tables(pos0 + jnp.arange(seq, dtype=F32))
    rb0 = row0 // rows_per_step
    if carry:
        tblocks = seq // rows_per_step
        grid = (batch, h_, tblocks)
        row_idx = lambda b, h, t: rb0 + b * tblocks + t
        rope_spec = pl.BlockSpec((rows_per_step, d // 2), lambda b, h, t: (t, 0))
        sout_spec = pl.BlockSpec((None, None, d, d), lambda b, h, t: (b, h, 0, 0))
    else:
        nb = rows_per_step // seq
        grid = (batch // nb, h_, 1)
        row_idx = lambda b, h, t: rb0 + b
        rope_spec = pl.BlockSpec((seq, d // 2), lambda b, h, t: (0, 0))
        sout_spec = pl.BlockSpec((nb, None, d, d), lambda b, h, t: (b, h, 0, 0))

    def col_spec(c0):
        return pl.BlockSpec((rows_per_step, d), lambda b, h, t: (row_idx(b, h, t), c0 + h))

    in_specs = [col_spec(0), col_spec(h_), col_spec(2 * h_), col_spec(3 * h_),
                rope_spec, rope_spec,
                pl.BlockSpec((None, L, L), lambda b, h, t: (h, 0, 0)),
                pl.BlockSpec((None, L, 1), lambda b, h, t: (h, 0, 0)),
                pl.BlockSpec((None, L, 1), lambda b, h, t: (h, 0, 0)),
                pl.BlockSpec((None, 1, 1), lambda b, h, t: (h, 0, 0))]
    args = [proj, proj, proj, proj, cos, sin, decay, q_decay, k_decay, chunk_decay]
    if not carry:
        in_specs.append(sout_spec)
        args.append(s0)
    aliases = _alias_existing(mixed, in_specs, args)
    return pl.pallas_call(
        functools.partial(_ret_kernel, L, units, carry, unroll),
        grid=grid,
        in_specs=in_specs,
        out_specs=[col_spec(0), sout_spec],
        out_shape=[jax.ShapeDtypeStruct(mixed.shape, mixed.dtype),
                   jax.ShapeDtypeStruct((batch, h_, d, d), F32)],
        input_output_aliases=aliases,
        compiler_params=_recurrence_cparams(),
        name="retention_prompt" if carry else "retention_sample",
    )(*args)


def _mlstm_kernel(L, units, carry, unroll, *refs):
    (q_ref, k_ref, v_ref, og_ref, gates_ref, bias_ref, nw_ref) = refs[:7]
    c0_ref, n0_ref, m0_ref = (None, None, None) if carry else refs[7:10]
    o_ref, c_ref, n_ref, m_ref = refs[-4:]

    if carry:
        @pl.when(pl.program_id(2) == 0)
        def _():
            c_ref[...] = jnp.zeros_like(c_ref)
            n_ref[...] = jnp.zeros_like(n_ref)
            m_ref[...] = jnp.zeros_like(m_ref)

    head = pl.program_id(1)
    bias = bias_ref[...]
    nw = nw_ref[...]
    lane = lax.broadcasted_iota(jnp.int32, (L, GATE_LANES), 1)
    ti = lax.broadcasted_iota(jnp.int32, (L, L), 0)
    si = lax.broadcasted_iota(jnp.int32, (L, L), 1)
    eye = ti == si
    causal = si <= ti
    scale = HEAD_DIM_AB ** -0.5

    def gate_terms(rows, m):
        x = gates_ref[rows, :] + bias
        ig = jnp.sum(jnp.where(lane == head, x, 0.0), axis=1, keepdims=True)
        fpre = jnp.sum(jnp.where(lane == head + ML_HEADS, x, 0.0), axis=1, keepdims=True)
        lf = jnp.minimum(fpre, 0.0) - jnp.log1p(jnp.exp(-jnp.abs(fpre)))
        f_cum = jnp.sum(jnp.where(causal, _col_to_row(lf, eye), 0.0), axis=1, keepdims=True)
        a = ig - f_cum
        a_row = _col_to_row(a, eye)
        cmax = jnp.max(jnp.where(causal, a_row, -jnp.inf), axis=1, keepdims=True)
        m_t = f_cum + jnp.maximum(m, cmax)
        f_end = f_cum[L - 1:L, :]
        m_end = m_t[L - 1:L, :]
        return dict(
            d_mat=jnp.exp(jnp.where(causal, (f_cum - m_t) + a_row, -jnp.inf)),
            inter=jnp.exp(f_cum + m - m_t),
            floor=jnp.exp(-m_t),
            w=jnp.exp(f_end - m_end + a),
            cdec=jnp.exp(f_end + m - m_end),
            m_end=m_end)

    def gate_free_products(rows, c):
        q = q_ref[rows, :]
        return _dot_nt(q, k_ref[rows, :] * scale), (None if c is None else _dot(q, c))

    def mix(rows, g, c, n, products):
        q = q_ref[rows, :]
        k = k_ref[rows, :] * scale
        v = v_ref[rows, :]
        qk, qc = products
        if qc is None:
            qc = _dot(q, c)
        s = qk * g["d_mat"]
        num = g["inter"] * qc + _dot(s, v)
        den = (g["inter"] * jnp.sum(q * n, axis=1, keepdims=True)
               + jnp.sum(s, axis=1, keepdims=True))
        hh = num * (1.0 / jnp.maximum(jnp.abs(den), g["floor"]))
        kw = k * g["w"]
        c_new = g["cdec"] * c + _dot_tn(kw, v)
        n_new = g["cdec"] * n + jnp.sum(kw, axis=0, keepdims=True)
        out = jax.nn.sigmoid(og_ref[rows, :]) * (_head_norm(hh) * nw)
        o_ref[rows, :] = out.astype(o_ref.dtype)
        return c_new, n_new

    if carry:
        def group(i, m):
            chunks = []
            for j in range(unroll):
                rows = pl.ds(pl.multiple_of((i * unroll + j) * L, L), L)
                g = gate_terms(rows, m)
                m = g["m_end"]
                chunks.append((rows, g))
            for rows, g in chunks:
                c_ref[...], n_ref[...] = mix(rows, g, c_ref[...], n_ref[...],
                                             gate_free_products(rows, None))
            return m

        m_last = lax.fori_loop(0, units // unroll, group, m_ref[:, 0:1])
        m_ref[...] = jnp.broadcast_to(m_last, m_ref.shape)
    else:
        early =[gate_free_products(pl.ds(u * L, L), c0_ref[u]) for u in range(units)]
        terms = [gate_terms(pl.ds(u * L, L), m0_ref[u][:, 0:1]) for u in range(units)]
        for u in range(units):
            rows = pl.ds(u * L, L)
            g = terms[u]
            c_ref[u], n_ref[u] = mix(rows, g, c0_ref[u], n0_ref[u], early[u])
            m_ref[u] = jnp.broadcast_to(g["m_end"], (1, GATE_LANES))


def _mlstm(proj, gates, bias, norm_w, mixed, *, row0, batch, seq, L, rows_per_step,
           c0=None, n0=None, m0=None, unroll=1):
    d = HEAD_DIM_AB
    h_ = ML_HEADS
    carry = c0 is None
    units = rows_per_step // L
    rb0 = row0 // rows_per_step
    col0 = 4 * RET_HEADS
    if carry:
        tblocks = seq // rows_per_step
        grid = (batch, h_, tblocks)
        row_idx = lambda b, h, t: rb0 + b * tblocks + t
        lead = None
    else:
        nb = rows_per_step // seq
        grid = (batch // nb, h_, 1)
        row_idx = lambda b, h, t: rb0 + b
        lead = nb
    c_spec = pl.BlockSpec((lead, None, d, d), lambda b, h, t: (b, h, 0, 0))
    n_spec = pl.BlockSpec((lead, None, 1, d), lambda b, h, t: (b, h, 0, 0))
    m_spec = pl.BlockSpec((lead, None, 1, GATE_LANES), lambda b, h, t: (b, h, 0, 0))

    def col_spec(c_blk):
        return pl.BlockSpec((rows_per_step, d), lambda b, h, t: (row_idx(b, h, t), c_blk + h))

    in_specs = [col_spec(col0), col_spec(col0 + h_), col_spec(col0 + 2 * h_), col_spec(col0 + 3 * h_),
                pl.BlockSpec((rows_per_step, GATE_LANES), lambda b, h, t: (row_idx(b, h, t), 0)),
                pl.BlockSpec((1, GATE_LANES), lambda b, h, t: (0, 0)),
                pl.BlockSpec((1, d), lambda b, h, t: (0, h))]
    args = [proj, proj, proj, proj, gates, bias, norm_w]
    if not carry:
        in_specs += [c_spec, n_spec, m_spec]
        args += [c0, n0, m0]
    aliases = _alias_existing(mixed, in_specs, args)
    return pl.pallas_call(
        functools.partial(_mlstm_kernel, L, units, carry, unroll),
        grid=grid,
        in_specs=in_specs,
        out_specs=[col_spec(RET_HEADS), c_spec, n_spec, m_spec],
        out_shape=[jax.ShapeDtypeStruct(mixed.shape, mixed.dtype),
                   jax.ShapeDtypeStruct((batch, h_, d, d), F32),
                   jax.ShapeDtypeStruct((batch, h_, 1, d), F32),
                   jax.ShapeDtypeStruct((batch, h_, 1, GATE_LANES), F32)],
        input_output_aliases=aliases,
        compiler_params=_recurrence_cparams(),
        name="mlstm_prompt" if carry else "mlstm_sample",
    )(*args)


def _hgrn_kernel(L, units, carry, unroll, *refs):
    (q_ref, f_ref, i_ref, g_ref, lbl_ref, nw_ref) = refs[:6]
    rows_ref = refs[-1]
    if carry:
        s0_ref = None
        o_ref, sout_ref, st_ref = refs[-4:-1]
    else:
        s0_ref = refs[6]
        o_ref, sout_ref, kd_ref, eg_ref = refs[-5:-1]
        st_ref = None

    if carry:
        @pl.when(pl.program_id(2) == 0)
        def _():
            st_ref[...] = jnp.zeros_like(st_ref)

    d = HG_DIM
    lg = lbl_ref[...]
    e = jnp.exp(lg - jnp.max(lg, axis=0, keepdims=True))
    sm = e / jnp.sum(e, axis=0, keepdims=True)
    lb = (sm[0:1, :] + sm[1:2, :]) - sm[0:1, :]
    nw = nw_ref[...]
    groups = L // 8
    pairs = [(j, jp) for jp in range(groups - 1) for j in range(jp + 1, groups)]
    row8 = lax.broadcasted_iota(jnp.int32, (8, d), 0)
    lane = lax.broadcasted_iota(jnp.int32, (8, L), 1)
    if not carry:
        eye_d = (lax.broadcasted_iota(jnp.int32, (d, d), 0)
                 == lax.broadcasted_iota(jnp.int32, (d, d), 1))

    def unit(u, carry_val):
        rows = pl.ds(pl.multiple_of(u * L, L), L)
        q = jax.nn.silu(q_ref[rows, :])
        fg = lb + (1.0 - lb) * jax.nn.sigmoid(f_ref[rows, :])
        k = 1.0 - fg
        logf = jnp.log(fg)
        v = i_ref[rows, :]
        gg = []
        base = jnp.zeros((1, d), F32)
        for j in range(groups):
            cum = logf[8 * j:8 * j + 8, :]
            for sh in (1, 2, 4):
                cum = cum + jnp.where(row8 >= sh, pltpu.roll(cum, sh, 0), 0.0)
            cum = cum + base
            gg.append(cum)
            base = cum[7:8, :]
        g_end = base
        g_cum = jnp.concatenate(gg, axis=0)
        qg = [q[8 * j:8 * j + 8, :] for j in range(groups)]
        kg = [k[8 * j:8 * j + 8, :] for j in range(groups)]
        rows_ref[0, rows, :] = g_cum
        rows_ref[1, rows, :] = k
        row0 = pl.multiple_of(u * L, L)
        att = []
        for j in range(groups):
            a = jnp.zeros((8, L), F32)
            for r in range(8):
                g_r = rows_ref[0, pl.ds(row0 + 8 * j + r, 1), :]
                k_r = rows_ref[1, pl.ds(row0 + 8 * j + r, 1), :]
                diff = jnp.where(row8 >= r, gg[j] - g_r, -jnp.inf)
                col = jnp.sum(qg[j] * k_r * jnp.exp(diff), axis=1, keepdims=True)
                a = jnp.where(lane == 8 * j + r, col, a)
            att.append(a)
        if pairs:
            ends = [gg[j][7:8, :] for j in range(groups)]
            kt = jnp.concatenate([kg[j] * jnp.exp(ends[j] - gg[j]) for j in range(groups)], axis=0)
            qt = jnp.concatenate([qg[j] * jnp.exp(gg[j] - ends[jp]) for j, jp in pairs], axis=0)
            cross = _dot_nt(qt, kt)
            for idx, (j, jp) in enumerate(pairs):
                in_group = (lane >= 8 * jp) & (lane < 8 * jp + 8)
                att[j] = jnp.where(in_group, cross[8 * idx:8 * idx + 8, :], att[j])
        att_m = jnp.concatenate(att, axis=0)
        qs = q * jnp.exp(g_cum)
        kd = k * jnp.exp(g_end - g_cum)
        if carry:
            st = st_ref[...]
            o = _dot_nt(qs, st) + _dot(att_m, v)
            st_ref[...] = st * jnp.exp(g_end) + _dot_tn(v, kd)
        else:
            o = _dot(qs, s0_ref[u]) + _dot(att_m, v)
            kd_ref[rows, :] = kd
            eg_ref[pl.ds(u, 1), :] = jnp.exp(g_end)
        out = (_head_norm(o) * nw) * jax.nn.silu(g_ref[rows, :])
        o_ref[rows, :] = out.astype(o_ref.dtype)
        return carry_val

    lax.fori_loop(0, units, unit, 0, unroll=unroll)

    if carry:
        @pl.when(pl.program_id(2) == pl.num_programs(2) - 1)
        def _():
            sout_ref[...] = st_ref[...].T
    else:
        per_tile = d // L
        row_chunk = lax.broadcasted_iota(jnp.int32, (d, d), 0) // L
        for t in range(units // per_tile):
            kd_t = kd_ref[t * d:(t + 1) * d, :].T
            v_t = i_ref[t * d:(t + 1) * d, :]
            for c in range(per_tile):
                u = t * per_tile + c
                upd = _dot(kd_t, jnp.where(row_chunk == c, v_t, 0.0))
                sout_ref[u] = _row_to_col(eg_ref[u:u + 1, :], eye_d) * s0_ref[u] + upd


def _hgrn(proj, lb_logits, norm_w, mixed, *, row0, batch, seq, L, rows_per_step, s0=None, unroll=1):
    d = HG_DIM
    h_ = proj.shape[1] // (4 * d)
    carry = s0 is None
    units = rows_per_step // L
    rb0 = row0 // rows_per_step
    if carry:
        tblocks = seq // rows_per_step
        grid = (batch, h_, tblocks)
        row_idx = lambda b, h, t: rb0 + b * tblocks + t
        sout_spec = pl.BlockSpec((None, None, d, d), lambda b, h, t: (b, h, 0, 0))
        scratch = [pltpu.VMEM((d, d), F32), pltpu.VMEM((2, rows_per_step, d), F32)]
    else:
        nb = rows_per_step // seq
        grid = (batch // nb, h_, 1)
        row_idx = lambda b, h, t: rb0 + b
        sout_spec = pl.BlockSpec((nb, None, d, d), lambda b, h, t: (b, h, 0, 0))
        scratch = [pltpu.VMEM((rows_per_step, d), F32), pltpu.VMEM((units, d), F32),
                   pltpu.VMEM((2, rows_per_step, d), F32)]

    def col_spec(c0):
        return pl.BlockSpec((rows_per_step, d), lambda b, h, t: (row_idx(b, h, t), c0 + h))

    in_specs = [col_spec(0), col_spec(h_), col_spec(2 * h_), col_spec(3 * h_),
                pl.BlockSpec((2, d), lambda b, h, t: (0, h)),
                pl.BlockSpec((1, d), lambda b, h, t: (0, 0))]
    args = [proj, proj, proj, proj, lb_logits, norm_w]
    if not carry:
        in_specs.append(sout_spec)
        args.append(s0)
    aliases = _alias_existing(mixed, in_specs, args)
    return pl.pallas_call(
        functools.partial(_hgrn_kernel, L, units, carry, unroll),
        grid=grid,
        in_specs=in_specs,
        out_specs=[col_spec(0), sout_spec],
        out_shape=[jax.ShapeDtypeStruct(mixed.shape, mixed.dtype),
                   jax.ShapeDtypeStruct((batch, h_, d, d), F32)],
        scratch_shapes=scratch,
        input_output_aliases=aliases,
        compiler_params=_recurrence_cparams(),
        name="hgrn_prompt" if carry else "hgrn_sample",
    )(*args)


def _chunk_len(t, c):
    return c if t % c == 0 else t


def _swiglu_residual(x, xg, ssq, w_gate, w_up, w_down, layer, next_norm_w):
    act = _gate_up(xg, w_gate, w_up, layer, TM_GATE_UP, 256, _rms_scale(ssq, x.shape[1]))
    return _matmul_residual(act, w_down, layer, x, TM_DOWN, 256, single_buffer_lhs=True,
                            next_norm_w=next_norm_w, name="down_proj")


def kernel(x_prompt, x_sample, state_ret, state_mlstm_C, state_mlstm_n, state_mlstm_m, state_hgrn,
           norm_mix_w, w_in_ab, b_if_ab, ml_norm_w, w_out_ab, w_in_c, lb_logits, hg_norm_w, w_out_c,
           norm_ffn_w, w_gate, w_up, w_down, norm_final_w):
    bp, tp, d = x_prompt.shape
    bs, ts, _ = x_sample.shape
    mp, ms = bp * tp, bs * ts
    tm = TM_WIDE

    w_in_ab_b, w_out_ab_b = w_in_ab.astype(BF16), w_out_ab
    w_in_c_b, w_out_c_b = w_in_c, w_out_c
    w_gate_b, w_up_b, w_down_b = w_gate, w_up, w_down

    ab_main = 4 * RET_HEADS * HEAD_DIM_AB + 4 * ML_HEADS * HEAD_DIM_AB
    x, hn = _rmsnorm_stack([x_prompt.reshape(mp, d), x_sample.reshape(ms, d)], norm_mix_w[0])
    proj = _matmul(hn, w_in_ab_b, 0, tm, 512, n_out=ab_main, name="in_proj_ab")
    gates = _matmul(hn, w_in_ab_b, 0, tm, GATE_LANES, n_out=GATE_LANES,
                    col_block0=ab_main // GATE_LANES, name="in_proj_gates")
    bias = jnp.pad(b_if_ab[0].astype(F32), (0, GATE_LANES - 2 * ML_HEADS)).reshape(1, GATE_LANES)
    ml_w = ml_norm_w[0].astype(F32).reshape(1, -1)

    mixed = jax.ShapeDtypeStruct((mp + ms, d), BF16)
    lp = _chunk_len(tp, RET_CHUNK)
    ls = _chunk_len(ts, RET_CHUNK)
    mixed, ret_p = _retention(proj, mixed, row0=0, batch=bp, seq=tp, L=lp, rows_per_step=4 * lp,
                              pos0=0.0, unroll=UNROLL_AB_PROMPT)
    mixed, ret_s = _retention(proj, mixed, row0=mp, batch=bs, seq=ts, L=ls, rows_per_step=8 * ls,
                              pos0=float(PAST_LEN), s0=state_ret[0], unroll=UNROLL_AB_SAMPLE)
    lp = _chunk_len(tp, ML_CHUNK)
    ls = _chunk_len(ts, ML_CHUNK)
    mixed, mc_p, mn_p, mm_p = _mlstm(proj, gates, bias, ml_w, mixed, row0=0, batch=bp, seq=tp, L=lp,
                                     rows_per_step=4 * lp, unroll=UNROLL_AB_PROMPT)
    m0 =jnp.broadcast_to(state_mlstm_m[0][:, :, None, None], (bs, ML_HEADS, 1, GATE_LANES))
    mixed, mc_s, mn_s, mm_s = _mlstm(proj, gates, bias, ml_w, mixed, row0=mp, batch=bs, seq=ts, L=ls,
                                     rows_per_step=8 * ls, c0=state_mlstm_C[0],
                                     n0=state_mlstm_n[0][:, :, None, :], m0=m0, unroll=UNROLL_AB_SAMPLE)
    x, xg, ssq = _matmul_residual(mixed, w_out_ab_b, 0, x, TM_OUT, 512, next_norm_w=norm_ffn_w[0],
                                  name="out_proj_ab")
    x, xg, ssq = _swiglu_residual(x, xg, ssq, w_gate_b, w_up_b, w_down_b, 0, norm_mix_w[1])

    proj = _matmul(xg, w_in_c_b, 0, tm, 512, scale=_rms_scale(ssq, d), name="in_proj_c")
    mixed = jax.ShapeDtypeStruct((mp + ms, d), BF16)
    hg_w =hg_norm_w[0].astype(F32).reshape(1, -1)
    lbl = lb_logits.astype(F32)
    lp = _chunk_len(tp, HG_CHUNK)
    ls = _chunk_len(ts, HG_CHUNK)
    mixed, hg_p = _hgrn(proj, lbl, hg_w, mixed, row0=0, batch=bp, seq=tp, L=lp, rows_per_step=32 * lp,
                        unroll=UNROLL_HG_PROMPT)
    mixed, hg_s = _hgrn(proj, lbl, hg_w, mixed, row0=mp, batch=bs, seq=ts, L=ls, rows_per_step=32 * ls,
                        s0=state_hgrn[0], unroll=UNROLL_HG_SAMPLE)
    x, xg, ssq = _matmul_residual(mixed, w_out_c_b, 0, x, TM_OUT, 512, next_norm_w=norm_ffn_w[1],
                                  name="out_proj_c")
    x = _swiglu_residual(x, xg, ssq, w_gate_b, w_up_b, w_down_b, 1, None)

    y_prompt = _rmsnorm(x, norm_final_w, F32, row0=0, rows=mp).reshape(bp, tp, d)
    y_sample = _rmsnorm(x, norm_final_w, F32, row0=mp, rows=ms).reshape(bs, ts, d)
    return (y_prompt, y_sample,
            ret_p[None], mc_p[None], mn_p[:, :, 0, :][None], mm_p[:, :, 0, 0][None], hg_p[None],
            ret_s[None], mc_s[None], mn_s[:, :, 0, :][None], mm_s[:, :, 0, 0][None], hg_s[None])
```

```python
import functools

import numpy as np
import jax
import jax.numpy as jnp
from jax import lax
from jax.experimental import pallas as pl
from jax.experimental.pallas import tpu as pltpu

F32 = jnp.float32
BF16 = jnp.bfloat16

EPS = 1e-6
ROPE_BASE = 10000.0
PAST_LEN = 16384

RET_HEADS = 8
ML_HEADS = 8
HEAD_DIM_AB = 256
HG_DIM = 128
RET_CHUNK = 128
ML_CHUNK = 128
HG_CHUNK = 32
LANES = 128
GATE_LANES = LANES

V7X_VMEM_LIMIT_BYTES = 56 * 1024 * 1024

TM_WIDE = 1536
TM_GATE_UP = TM_WIDE
GATE_UP_ROW_CHUNKS = 6
TM_OUT = 1024
TM_DOWN = 1024

UNROLL_AB_PROMPT = 2
UNROLL_AB_SAMPLE = 8
UNROLL_HG_PROMPT = 32
UNROLL_HG_SAMPLE = 8


def _cparams(semantics, **kwargs):
    return pltpu.CompilerParams(dimension_semantics=semantics,
                                vmem_limit_bytes=V7X_VMEM_LIMIT_BYTES, **kwargs)


def _recurrence_cparams():
    return _cparams(("parallel", "parallel", "arbitrary"))


def _rmsnorm_kernel(x_ref, w_ref, o_ref):
    x = x_ref[...]
    y = x * lax.rsqrt(jnp.mean(x * x, axis=-1, keepdims=True) + EPS)
    o_ref[...] = (y * w_ref[...]).astype(o_ref.dtype)


def _rmsnorm(x, w, out_dtype, tm=512, row0=0, rows=None):
    d = x.shape[1]
    rows = x.shape[0] if rows is None else rows
    rb0 = row0 // tm
    return pl.pallas_call(
        _rmsnorm_kernel,
        grid=(rows // tm,),
        in_specs=[pl.BlockSpec((tm, d), lambda i: (rb0 + i, 0)),
                  pl.BlockSpec((1, d), lambda i: (0, 0))],
        out_specs=pl.BlockSpec((tm, d), lambda i: (i, 0)),
        out_shape=jax.ShapeDtypeStruct((rows, d), out_dtype),
        compiler_params=_cparams(("parallel",)),
        name="rmsnorm",
    )(x, w.reshape(1, d).astype(F32))


def _rmsnorm_stack_kernel(x_ref, w_ref, *rest):
    xo_ref, ho_ref = rest[-2:]
    x = x_ref[...]
    xo_ref[...] = x
    y = x * lax.rsqrt(jnp.mean(x * x, axis=-1, keepdims=True) + EPS)
    ho_ref[...] = (y * w_ref[...]).astype(ho_ref.dtype)


def _rmsnorm_stack(parts, w, tm=512):
    d = parts[0].shape[1]
    m = sum(p.shape[0] for p in parts)
    out_shape = [jax.ShapeDtypeStruct((m, d), F32), jax.ShapeDtypeStruct((m, d), BF16)]
    w2 = w.reshape(1, d).astype(F32)
    outs = None
    row0 = 0
    for part in parts:
        rb0 = row0 // tm
        out_spec = pl.BlockSpec((tm, d), lambda i, rb0=rb0: (rb0 + i, 0))
        in_specs = [pl.BlockSpec((tm, d), lambda i: (i, 0)), pl.BlockSpec((1, d), lambda i: (0, 0))]
        args = [part, w2]
        aliases = {}
        if outs is not None:
            in_specs += [pl.BlockSpec(memory_space=pl.ANY)] * 2
            args += list(outs)
            aliases = {2: 0, 3: 1}
        outs = pl.pallas_call(
            _rmsnorm_stack_kernel,
            grid=(part.shape[0] // tm,),
            in_specs=in_specs,
            out_specs=[out_spec, out_spec],
            out_shape=out_shape,
            input_output_aliases=aliases,
            compiler_params=_cparams(("parallel",)),
            name="rmsnorm_stack",
        )(*args)
        row0 += part.shape[0]
    return outs


def _mm_kernel(x_ref, w_ref, o_ref):
    w = w_ref[...].astype(BF16)
    o_ref[...] = jnp.dot(x_ref[...], w, preferred_element_type=F32).astype(o_ref.dtype)


def _mm_residual_kernel(x_ref, w_ref, r_ref, o_ref):
    w = w_ref[...].astype(BF16)
    o_ref[...] = r_ref[...] + jnp.dot(x_ref[...], w, preferred_element_type=F32)


def _gate_up_kernel(x_ref, wg_ref, wu_ref, o_ref):
    x = x_ref[...]
    a = jnp.dot(x, wg_ref[...].astype(BF16), preferred_element_type=F32)
    b = jnp.dot(x, wu_ref[...].astype(BF16), preferred_element_type=F32)
    o_ref[...] = (jax.nn.silu(a) * b).astype(o_ref.dtype)


def _mm_residual_prenorm_kernel(x_ref, w_ref, r_ref, g_ref, o_ref, yg_ref, ssq_ref):
    w = w_ref[...].astype(BF16)
    y = r_ref[...] + jnp.dot(x_ref[...], w, preferred_element_type=F32)
    o_ref[...] = y
    yg_ref[...] = (y * g_ref[...]).astype(yg_ref.dtype)
    sq = y * y
    part = sq[:, :LANES]
    for c in range(1, sq.shape[1] // LANES):
        part = part + sq[:, c * LANES:(c + 1) * LANES]

    @pl.when(pl.program_id(1) == 0)
    def _():
        ssq_ref[...] = part

    @pl.when(pl.program_id(1) > 0)
    def _():
        ssq_ref[...] += part


def _rms_scale_kernel(d_model, ssq_ref, o_ref):
    ms = jnp.sum(ssq_ref[...], axis=-1, keepdims=True) / d_model
    o_ref[...] = jnp.broadcast_to(lax.rsqrt(ms + EPS), o_ref.shape)


def _rms_scale(ssq, d_model, tm=1024):
    m = ssq.shape[0]
    spec = pl.BlockSpec((tm, LANES), lambda i: (i, 0))
    return pl.pallas_call(
        functools.partial(_rms_scale_kernel, d_model),
        grid=(m // tm,),
        in_specs=[spec],
        out_specs=spec,
        out_shape=jax.ShapeDtypeStruct((m, LANES), F32),
        compiler_params=_cparams(("parallel",)),
        name="rms_scale",
    )(ssq)


def _lane_tiled(scale_ref, width):
    r = scale_ref[...]
    return jnp.concatenate([r] * (width // LANES), axis=1)


def _mm_postnorm_kernel(x_ref, w_ref, scale_ref, o_ref):
    r = _lane_tiled(scale_ref, o_ref.shape[1])
    w = w_ref[...].astype(BF16)
    acc = jnp.dot(x_ref[...], w, preferred_element_type=F32)
    o_ref[...] = (r * acc).astype(o_ref.dtype)


def _gate_up_postnorm_kernel(x_ref, wg_ref, wu_ref, scale_ref, o_ref):
    tm, tn = o_ref.shape
    wg = wg_ref[...].astype(BF16)
    wu = wu_ref[...].astype(BF16)
    rc = tm // GATE_UP_ROW_CHUNKS
    for c in range(GATE_UP_ROW_CHUNKS):
        rows = pl.ds(c * rc, rc)
        r = _lane_tiled(scale_ref.at[rows, :], tn)
        x = x_ref[rows, :]
        a = r * jnp.dot(x, wg, preferred_element_type=F32)
        b = r * jnp.dot(x, wu, preferred_element_type=F32)
        o_ref[rows, :] = (jax.nn.silu(a) * b).astype(o_ref.dtype)


def _weight_spec(w, layer, tn, col_block0=0):
    return pl.BlockSpec((None, w.shape[1], tn), lambda i, j: (layer, 0, col_block0 + j))


def _matmul(x, w, layer, tm, tn, n_out=None, col_block0=0, scale=None, out_dtype=F32, name="matmul"):
    m, k = x.shape
    n = w.shape[2] if n_out is None else n_out
    in_specs = [pl.BlockSpec((tm, k), lambda i, j: (i, 0)),
                _weight_spec(w, layer, tn, col_block0)]
    args = [x, w]
    body = _mm_kernel
    if scale is not None:
        in_specs.append(pl.BlockSpec((tm, LANES), lambda i, j: (i, 0)))
        args.append(scale)
        body = _mm_postnorm_kernel
    return pl.pallas_call(
        body,
        grid=(m // tm, n // tn),
        in_specs=in_specs,
        out_specs=pl.BlockSpec((tm, tn), lambda i, j: (i, j)),
        out_shape=jax.ShapeDtypeStruct((m, n), out_dtype),
        compiler_params=_cparams(("parallel", "arbitrary")),
        name=name,
    )(*args)


def _matmul_residual(x, w, layer, res, tm, tn, single_buffer_lhs=False, next_norm_w=None,
                     name="matmul_residual"):
    m, k = x.shape
    n = w.shape[2]
    lhs_kwargs = {"pipeline_mode": pl.Buffered(1)} if single_buffer_lhs else {}
    tile = pl.BlockSpec((tm, tn), lambda i, j: (i, j))
    in_specs = [pl.BlockSpec((tm, k), lambda i, j: (i, 0), **lhs_kwargs),
                _weight_spec(w, layer, tn), tile]
    args = [x, w, res]
    if next_norm_w is None:
        body, out_specs, out_shape = _mm_residual_kernel, tile, jax.ShapeDtypeStruct((m, n), F32)
    else:
        in_specs.append(pl.BlockSpec((1, tn), lambda i, j: (0, j)))
        args.append(next_norm_w.reshape(1, n).astype(F32))
        body = _mm_residual_prenorm_kernel
        out_specs = [tile, tile, pl.BlockSpec((tm, LANES), lambda i, j: (i, 0))]
        out_shape = [jax.ShapeDtypeStruct((m, n), F32), jax.ShapeDtypeStruct((m, n), BF16),
                     jax.ShapeDtypeStruct((m, LANES), F32)]
    return pl.pallas_call(
        body,
        grid=(m // tm, n // tn),
        in_specs=in_specs,
        out_specs=out_specs,
        out_shape=out_shape,
        compiler_params=_cparams(("parallel", "arbitrary")),
        name=name,
    )(*args)


def _gate_up(x, wg, wu, layer, tm, tn, scale):
    m, k = x.shape
    n = wg.shape[2]
    return pl.pallas_call(
        _gate_up_postnorm_kernel,
        grid=(m // tm, n // tn),
        in_specs=[pl.BlockSpec((tm, k), lambda i, j: (i, 0)),
                  _weight_spec(wg, layer, tn),
                  _weight_spec(wu, layer, tn),
                  pl.BlockSpec((tm, LANES), lambda i, j: (i, 0))],
        out_specs=pl.BlockSpec((tm, tn), lambda i, j: (i, j)),
        out_shape=jax.ShapeDtypeStruct((m, n), BF16),
        compiler_params=_cparams(("parallel", "arbitrary")),
        name="gate_up",
    )(x, wg, wu, scale)


def _dot_nt(a, b):
    return lax.dot_general(a, b, (((1,), (1,)), ((), ())), preferred_element_type=F32)


def _dot_tn(a, b):
    return lax.dot_general(a, b, (((0,), (0,)), ((), ())), preferred_element_type=F32)


def _dot(a, b):
    return jnp.dot(a, b, preferred_element_type=F32)


def _head_norm(x):
    return x * lax.rsqrt(jnp.mean(x * x, axis=-1, keepdims=True) + EPS)


def _col_to_row(col, eye):
    return jnp.sum(jnp.where(eye, col, 0.0), axis=0, keepdims=True)


def _row_to_col(row, eye):
    return jnp.sum(jnp.where(eye, row, 0.0), axis=1, keepdims=True)


def _alias_existing(mixed, in_specs, args):
    if isinstance(mixed, jax.ShapeDtypeStruct):
        return {}
    in_specs.append(pl.BlockSpec(memory_space=pl.ANY))
    args.append(mixed)
    return {len(args) - 1: 0}


def _rotary(x, cos, sin):
    half = x.shape[-1] // 2
    x1, x2 = x[:, :half], x[:, half:]
    return jnp.concatenate([x1 * cos - x2 * sin, x1 * sin + x2 * cos], axis=-1)


def _ret_kernel(L, units, carry, unroll, *refs):
    (q_ref, k_ref, v_ref, g_ref, cos_ref, sin_ref, dec_ref, qd_ref, kd_ref, cd_ref) = refs[:10]
    s0_ref = None if carry else refs[10]
    o_ref, sout_ref = refs[-2:]

    if carry:
        @pl.when(pl.program_id(2) == 0)
        def _():
            sout_ref[...] = jnp.zeros_like(sout_ref)

    dec = dec_ref[...]
    qd = qd_ref[...]
    kd = kd_ref[...]
    cd = cd_ref[...]
    scale = HEAD_DIM_AB ** -0.5

    def unit(u, c):
        rows = pl.ds(pl.multiple_of(u * L, L), L)
        if carry:
            cos, sin = cos_ref[rows, :], sin_ref[rows, :]
            s = sout_ref[...]
        else:
            cos, sin = cos_ref[...], sin_ref[...]
            s = s0_ref[u]
        q = _rotary(q_ref[rows, :], cos, sin)
        k = _rotary(k_ref[rows, :], cos, sin) * scale
        v = v_ref[rows, :]
        att = _dot_nt(q, k) * dec
        o = _dot(att, v) + _dot(q, s) * qd
        s_new = cd * s + _dot_tn(k * kd, v)
        out = jax.nn.silu(g_ref[rows, :]) * _head_norm(o)
        o_ref[rows, :] = out.astype(o_ref.dtype)
        if carry:
            sout_ref[...] = s_new
        else:
            sout_ref[u] = s_new
        return c

    lax.fori_loop(0, units, unit, 0, unroll=unroll)


def _retention_tables(L):
    log_gamma = jnp.log1p(-jnp.exp2(-5.0 - jnp.arange(RET_HEADS, dtype=F32)))[:, None]
    idx = jnp.arange(L, dtype=F32)
    diff = idx[:, None] - idx[None, :]
    decay = jnp.exp(jnp.where(diff >= 0, log_gamma[:, :, None] * diff, -jnp.inf))
    q_decay = jnp.exp(log_gamma * (idx + 1.0))[:, :, None]
    k_decay = jnp.exp(log_gamma * (L - 1.0 - idx))[:, :, None]
    chunk_decay = jnp.exp(log_gamma[:, 0] * L)[:, None, None]
    return decay, q_decay, k_decay, chunk_decay


def _rope_tables(pos):
    half = HEAD_DIM_AB // 2
    inv_freq = ROPE_BASE ** (-jnp.arange(half, dtype=F32) / half)
    ang = pos[:, None] * inv_freq[None, :]
    return jnp.cos(ang), jnp.sin(ang)


def _retention(proj, mixed, *, row0, batch, seq, L, rows_per_step, pos0, s0=None, unroll=1):
    d = HEAD_DIM_AB
    h_ = RET_HEADS
    carry = s0 is None
    units = rows_per_step // L
    decay, q_decay, k_decay, chunk_decay = _retention_tables(L)
    cos, sin = _rope_tables(pos0 + jnp.arange(seq, dtype=F32))
    rb0 = row0 // rows_per_step
    if carry:
        tblocks = seq // rows_per_step
        grid = (batch, h_, tblocks)
        row_idx = lambda b, h, t: rb0 + b * tblocks + t
        rope_spec = pl.BlockSpec((rows_per_step, d // 2), lambda b, h, t: (t, 0))
        sout_spec = pl.BlockSpec((None, None, d, d), lambda b, h, t: (b, h, 0, 0))
    else:
        nb = rows_per_step // seq
        grid = (batch // nb, h_, 1)
        row_idx = lambda b, h, t: rb0 + b
        rope_spec = pl.BlockSpec((seq, d // 2), lambda b, h, t: (0, 0))
        sout_spec = pl.BlockSpec((nb, None, d, d), lambda b, h, t: (b, h, 0, 0))

    def col_spec(c0):
        return pl.BlockSpec((rows_per_step, d), lambda b, h, t: (row_idx(b, h, t), c0 + h))

    in_specs = [col_spec(0), col_spec(h_), col_spec(2 * h_), col_spec(3 * h_),
                rope_spec, rope_spec,
                pl.BlockSpec((None, L, L), lambda b, h, t: (h, 0, 0)),
                pl.BlockSpec((None, L, 1), lambda b, h, t: (h, 0, 0)),
                pl.BlockSpec((None, L, 1), lambda b, h, t: (h, 0, 0)),
                pl.BlockSpec((None, 1, 1), lambda b, h, t: (h, 0, 0))]
    args = [proj, proj, proj, proj, cos, sin, decay, q_decay, k_decay, chunk_decay]
    if not carry:
        in_specs.append(sout_spec)
        args.append(s0)
    aliases = _alias_existing(mixed, in_specs, args)
    return pl.pallas_call(
        functools.partial(_ret_kernel, L, units, carry, unroll),
        grid=grid,
        in_specs=in_specs,
        out_specs=[col_spec(0), sout_spec],
        out_shape=[jax.ShapeDtypeStruct(mixed.shape, mixed.dtype),
                   jax.ShapeDtypeStruct((batch, h_, d, d), F32)],
        input_output_aliases=aliases,
        compiler_params=_recurrence_cparams(),
        name="retention_prompt" if carry else "retention_sample",
    )(*args)


def _mlstm_kernel(L, units, carry, unroll, *refs):
    (q_ref, k_ref, v_ref, og_ref, gates_ref, bias_ref, nw_ref) = refs[:7]
    c0_ref, n0_ref, m0_ref = (None, None, None) if carry else refs[7:10]
    o_ref, c_ref, n_ref, m_ref = refs[-4:]

    if carry:
        @pl.when(pl.program_id(2) == 0)
        def _():
            c_ref[...] = jnp.zeros_like(c_ref)
            n_ref[...] = jnp.zeros_like(n_ref)
            m_ref[...] = jnp.zeros_like(m_ref)

    head = pl.program_id(1)
    bias = bias_ref[...]
    nw = nw_ref[...]
    lane = lax.broadcasted_iota(jnp.int32, (L, GATE_LANES), 1)
    ti = lax.broadcasted_iota(jnp.int32, (L, L), 0)
    si = lax.broadcasted_iota(jnp.int32, (L, L), 1)
    eye = ti == si
    causal = si <= ti
    scale = HEAD_DIM_AB ** -0.5

    def gate_terms(rows, m):
        x = gates_ref[rows, :] + bias
        ig = jnp.sum(jnp.where(lane == head, x, 0.0), axis=1, keepdims=True)
        fpre = jnp.sum(jnp.where(lane == head + ML_HEADS, x, 0.0), axis=1, keepdims=True)
        lf = jnp.minimum(fpre, 0.0) - jnp.log1p(jnp.exp(-jnp.abs(fpre)))
        f_cum = jnp.sum(jnp.where(causal, _col_to_row(lf, eye), 0.0), axis=1, keepdims=True)
        a = ig - f_cum
        a_row = _col_to_row(a, eye)
        cmax = jnp.max(jnp.where(causal, a_row, -jnp.inf), axis=1, keepdims=True)
        m_t = f_cum + jnp.maximum(m, cmax)
        f_end = f_cum[L - 1:L, :]
        m_end = m_t[L - 1:L, :]
        return dict(
            d_mat=jnp.exp(jnp.where(causal, (f_cum - m_t) + a_row, -jnp.inf)),
            inter=jnp.exp(f_cum + m - m_t),
            floor=jnp.exp(-m_t),
            w=jnp.exp(f_end - m_end + a),
            cdec=jnp.exp(f_end + m - m_end),
            m_end=m_end)

    def gate_free_products(rows, c):
        q = q_ref[rows, :]
        return _dot_nt(q, k_ref[rows, :] * scale), (None if c is None else _dot(q, c))

    def mix(rows, g, c, n, products):
        q = q_ref[rows, :]
        k = k_ref[rows, :] * scale
        v = v_ref[rows, :]
        qk, qc = products
        if qc is None:
            qc = _dot(q, c)
        s = qk * g["d_mat"]
        num = g["inter"] * qc + _dot(s, v)
        den = (g["inter"] * jnp.sum(q * n, axis=1, keepdims=True)
               + jnp.sum(s, axis=1, keepdims=True))
        hh = num * (1.0 / jnp.maximum(jnp.abs(den), g["floor"]))
        kw = k * g["w"]
        c_new = g["cdec"] * c + _dot_tn(kw, v)
        n_new = g["cdec"] * n + jnp.sum(kw, axis=0, keepdims=True)
        out = jax.nn.sigmoid(og_ref[rows, :]) * (_head_norm(hh) * nw)
        o_ref[rows, :] = out.astype(o_ref.dtype)
        return c_new, n_new

    if carry:
        def group(i, m):
            chunks = []
            for j in range(unroll):
                rows = pl.ds(pl.multiple_of((i * unroll + j) * L, L), L)
                g = gate_terms(rows, m)
                m = g["m_end"]
                chunks.append((rows, g))
            for rows, g in chunks:
                c_ref[...], n_ref[...] = mix(rows, g, c_ref[...], n_ref[...],
                                             gate_free_products(rows, None))
            return m

        m_last = lax.fori_loop(0, units // unroll, group, m_ref[:, 0:1])
        m_ref[...] = jnp.broadcast_to(m_last, m_ref.shape)
    else:
        early =[gate_free_products(pl.ds(u * L, L), c0_ref[u]) for u in range(units)]
        terms = [gate_terms(pl.ds(u * L, L), m0_ref[u][:, 0:1]) for u in range(units)]
        for u in range(units):
            rows = pl.ds(u * L, L)
            g = terms[u]
            c_ref[u], n_ref[u] = mix(rows, g, c0_ref[u], n0_ref[u], early[u])
            m_ref[u] = jnp.broadcast_to(g["m_end"], (1, GATE_LANES))


def _mlstm(proj, gates, bias, norm_w, mixed, *, row0, batch, seq, L, rows_per_step,
           c0=None, n0=None, m0=None, unroll=1):
    d = HEAD_DIM_AB
    h_ = ML_HEADS
    carry = c0 is None
    units = rows_per_step // L
    rb0 = row0 // rows_per_step
    col0 = 4 * RET_HEADS
    if carry:
        tblocks = seq // rows_per_step
        grid = (batch, h_, tblocks)
        row_idx = lambda b, h, t: rb0 + b * tblocks + t
        lead = None
    else:
        nb = rows_per_step // seq
        grid = (batch // nb, h_, 1)
        row_idx = lambda b, h, t: rb0 + b
        lead = nb
    c_spec = pl.BlockSpec((lead, None, d, d), lambda b, h, t: (b, h, 0, 0))
    n_spec = pl.BlockSpec((lead, None, 1, d), lambda b, h, t: (b, h, 0, 0))
    m_spec = pl.BlockSpec((lead, None, 1, GATE_LANES), lambda b, h, t: (b, h, 0, 0))

    def col_spec(c_blk):
        return pl.BlockSpec((rows_per_step, d), lambda b, h, t: (row_idx(b, h, t), c_blk + h))

    in_specs = [col_spec(col0), col_spec(col0 + h_), col_spec(col0 + 2 * h_), col_spec(col0 + 3 * h_),
                pl.BlockSpec((rows_per_step, GATE_LANES), lambda b, h, t: (row_idx(b, h, t), 0)),
                pl.BlockSpec((1, GATE_LANES), lambda b, h, t: (0, 0)),
                pl.BlockSpec((1, d), lambda b, h, t: (0, h))]
    args = [proj, proj, proj, proj, gates, bias, norm_w]
    if not carry:
        in_specs += [c_spec, n_spec, m_spec]
        args += [c0, n0, m0]
    aliases = _alias_existing(mixed, in_specs, args)
    return pl.pallas_call(
        functools.partial(_mlstm_kernel, L, units, carry, unroll),
        grid=grid,
        in_specs=in_specs,
        out_specs=[col_spec(RET_HEADS), c_spec, n_spec, m_spec],
        out_shape=[jax.ShapeDtypeStruct(mixed.shape, mixed.dtype),
                   jax.ShapeDtypeStruct((batch, h_, d, d), F32),
                   jax.ShapeDtypeStruct((batch, h_, 1, d), F32),
                   jax.ShapeDtypeStruct((batch, h_, 1, GATE_LANES), F32)],
        input_output_aliases=aliases,
        compiler_params=_recurrence_cparams(),
        name="mlstm_prompt" if carry else "mlstm_sample",
    )(*args)


def _hgrn_kernel(L, units, carry, unroll, *refs):
    (q_ref, f_ref, i_ref, g_ref, lbl_ref, nw_ref) = refs[:6]
    rows_ref = refs[-1]
    if carry:
        s0_ref = None
        o_ref, sout_ref, st_ref = refs[-4:-1]
    else:
        s0_ref = refs[6]
        o_ref, sout_ref, kd_ref, eg_ref = refs[-5:-1]
        st_ref = None

    if carry:
        @pl.when(pl.program_id(2) == 0)
        def _():
            st_ref[...] = jnp.zeros_like(st_ref)

    d = HG_DIM
    lg = lbl_ref[...]
    e = jnp.exp(lg - jnp.max(lg, axis=0, keepdims=True))
    sm = e / jnp.sum(e, axis=0, keepdims=True)
    lb = (sm[0:1, :] + sm[1:2, :]) - sm[0:1, :]
    nw = nw_ref[...]
    groups = L // 8
    pairs = [(j, jp) for jp in range(groups - 1) for j in range(jp + 1, groups)]
    row8 = lax.broadcasted_iota(jnp.int32, (8, d), 0)
    lane = lax.broadcasted_iota(jnp.int32, (8, L), 1)
    if not carry:
        eye_d = (lax.broadcasted_iota(jnp.int32, (d, d), 0)
                 == lax.broadcasted_iota(jnp.int32, (d, d), 1))

    def unit(u, carry_val):
        rows = pl.ds(pl.multiple_of(u * L, L), L)
        q = jax.nn.silu(q_ref[rows, :])
        fg = lb + (1.0 - lb) * jax.nn.sigmoid(f_ref[rows, :])
        k = 1.0 - fg
        logf = jnp.log(fg)
        v = i_ref[rows, :]
        gg = []
        base = jnp.zeros((1, d), F32)
        for j in range(groups):
            cum = logf[8 * j:8 * j + 8, :]
            for sh in (1, 2, 4):
                cum = cum + jnp.where(row8 >= sh, pltpu.roll(cum, sh, 0), 0.0)
            cum = cum + base
            gg.append(cum)
            base = cum[7:8, :]
        g_end = base
        g_cum = jnp.concatenate(gg, axis=0)
        qg = [q[8 * j:8 * j + 8, :] for j in range(groups)]
        kg = [k[8 * j:8 * j + 8, :] for j in range(groups)]
        rows_ref[0, rows, :] = g_cum
        rows_ref[1, rows, :] = k
        row0 = pl.multiple_of(u * L, L)
        att = []
        for j in range(groups):
            a = jnp.zeros((8, L), F32)
            for r in range(8):
                g_r = rows_ref[0, pl.ds(row0 + 8 * j + r, 1), :]
                k_r = rows_ref[1, pl.ds(row0 + 8 * j + r, 1), :]
                diff = jnp.where(row8 >= r, gg[j] - g_r, -jnp.inf)
                col = jnp.sum(qg[j] * k_r * jnp.exp(diff), axis=1, keepdims=True)
                a = jnp.where(lane == 8 * j + r, col, a)
            att.append(a)
        if pairs:
            ends = [gg[j][7:8, :] for j in range(groups)]
            kt = jnp.concatenate([kg[j] * jnp.exp(ends[j] - gg[j]) for j in range(groups)], axis=0)
            qt = jnp.concatenate([qg[j] * jnp.exp(gg[j] - ends[jp]) for j, jp in pairs], axis=0)
            cross = _dot_nt(qt, kt)
            for idx, (j, jp) in enumerate(pairs):
                in_group = (lane >= 8 * jp) & (lane < 8 * jp + 8)
                att[j] = jnp.where(in_group, cross[8 * idx:8 * idx + 8, :], att[j])
        att_m = jnp.concatenate(att, axis=0)
        qs = q * jnp.exp(g_cum)
        kd = k * jnp.exp(g_end - g_cum)
        if carry:
            st = st_ref[...]
            o = _dot_nt(qs, st) + _dot(att_m, v)
            st_ref[...] = st * jnp.exp(g_end) + _dot_tn(v, kd)
        else:
            o = _dot(qs, s0_ref[u]) + _dot(att_m, v)
            kd_ref[rows, :] = kd
            eg_ref[pl.ds(u, 1), :] = jnp.exp(g_end)
        out = (_head_norm(o) * nw) * jax.nn.silu(g_ref[rows, :])
        o_ref[rows, :] = out.astype(o_ref.dtype)
        return carry_val

    lax.fori_loop(0, units, unit, 0, unroll=unroll)

    if carry:
        @pl.when(pl.program_id(2) == pl.num_programs(2) - 1)
        def _():
            sout_ref[...] = st_ref[...].T
    else:
        per_tile = d // L
        row_chunk = lax.broadcasted_iota(jnp.int32, (d, d), 0) // L
        for t in range(units // per_tile):
            kd_t = kd_ref[t * d:(t + 1) * d, :].T
            v_t = i_ref[t * d:(t + 1) * d, :]
            for c in range(per_tile):
                u = t * per_tile + c
                upd = _dot(kd_t, jnp.where(row_chunk == c, v_t, 0.0))
                sout_ref[u] = _row_to_col(eg_ref[u:u + 1, :], eye_d) * s0_ref[u] + upd


def _hgrn(proj, lb_logits, norm_w, mixed, *, row0, batch, seq, L, rows_per_step, s0=None, unroll=1):
    d = HG_DIM
    h_ = proj.shape[1] // (4 * d)
    carry = s0 is None
    units = rows_per_step // L
    rb0 = row0 // rows_per_step
    if carry:
        tblocks = seq // rows_per_step
        grid = (batch, h_, tblocks)
        row_idx = lambda b, h, t: rb0 + b * tblocks + t
        sout_spec = pl.BlockSpec((None, None, d, d), lambda b, h, t: (b, h, 0, 0))
        scratch = [pltpu.VMEM((d, d), F32), pltpu.VMEM((2, rows_per_step, d), F32)]
    else:
        nb = rows_per_step // seq
        grid = (batch // nb, h_, 1)
        row_idx = lambda b, h, t: rb0 + b
        sout_spec = pl.BlockSpec((nb, None, d, d), lambda b, h, t: (b, h, 0, 0))
        scratch = [pltpu.VMEM((rows_per_step, d), F32), pltpu.VMEM((units, d), F32),
                   pltpu.VMEM((2, rows_per_step, d), F32)]

    def col_spec(c0):
        return pl.BlockSpec((rows_per_step, d), lambda b, h, t: (row_idx(b, h, t), c0 + h))

    in_specs = [col_spec(0), col_spec(h_), col_spec(2 * h_), col_spec(3 * h_),
                pl.BlockSpec((2, d), lambda b, h, t: (0, h)),
                pl.BlockSpec((1, d), lambda b, h, t: (0, 0))]
    args = [proj, proj, proj, proj, lb_logits, norm_w]
    if not carry:
        in_specs.append(sout_spec)
        args.append(s0)
    aliases = _alias_existing(mixed, in_specs, args)
    return pl.pallas_call(
        functools.partial(_hgrn_kernel, L, units, carry, unroll),
        grid=grid,
        in_specs=in_specs,
        out_specs=[col_spec(0), sout_spec],
        out_shape=[jax.ShapeDtypeStruct(mixed.shape, mixed.dtype),
                   jax.ShapeDtypeStruct((batch, h_, d, d), F32)],
        scratch_shapes=scratch,
        input_output_aliases=aliases,
        compiler_params=_recurrence_cparams(),
        name="hgrn_prompt" if carry else "hgrn_sample",
    )(*args)


def _chunk_len(t, c):
    return c if t % c == 0 else t


def _swiglu_residual(x, xg, ssq, w_gate, w_up, w_down, layer, next_norm_w):
    act = _gate_up(xg, w_gate, w_up, layer, TM_GATE_UP, 256, _rms_scale(ssq, x.shape[1]))
    return _matmul_residual(act, w_down, layer, x, TM_DOWN, 256, single_buffer_lhs=True,
                            next_norm_w=next_norm_w, name="down_proj")


def kernel(x_prompt, x_sample, state_ret, state_mlstm_C, state_mlstm_n, state_mlstm_m, state_hgrn,
           norm_mix_w, w_in_ab, b_if_ab, ml_norm_w, w_out_ab, w_in_c, lb_logits, hg_norm_w, w_out_c,
           norm_ffn_w, w_gate, w_up, w_down, norm_final_w):
    bp, tp, d = x_prompt.shape
    bs, ts, _ = x_sample.shape
    mp, ms = bp * tp, bs * ts
    tm = TM_WIDE

    w_in_ab_b, w_out_ab_b = w_in_ab.astype(BF16), w_out_ab
    w_in_c_b, w_out_c_b = w_in_c, w_out_c
    w_gate_b, w_up_b, w_down_b = w_gate, w_up, w_down

    ab_main = 4 * RET_HEADS * HEAD_DIM_AB + 4 * ML_HEADS * HEAD_DIM_AB
    x, hn = _rmsnorm_stack([x_prompt.reshape(mp, d), x_sample.reshape(ms, d)], norm_mix_w[0])
    proj = _matmul(hn, w_in_ab_b, 0, tm, 512, n_out=ab_main, name="in_proj_ab")
    gates = _matmul(hn, w_in_ab_b, 0, tm, GATE_LANES, n_out=GATE_LANES,
                    col_block0=ab_main // GATE_LANES, name="in_proj_gates")
    bias = jnp.pad(b_if_ab[0].astype(F32), (0, GATE_LANES - 2 * ML_HEADS)).reshape(1, GATE_LANES)
    ml_w = ml_norm_w[0].astype(F32).reshape(1, -1)

    mixed = jax.ShapeDtypeStruct((mp + ms, d), BF16)
    lp = _chunk_len(tp, RET_CHUNK)
    ls = _chunk_len(ts, RET_CHUNK)
    mixed, ret_p = _retention(proj, mixed, row0=0, batch=bp, seq=tp, L=lp, rows_per_step=8 * lp,
                              pos0=0.0, unroll=UNROLL_AB_PROMPT)
    mixed, ret_s = _retention(proj, mixed, row0=mp, batch=bs, seq=ts, L=ls, rows_per_step=8 * ls,
                              pos0=float(PAST_LEN), s0=state_ret[0], unroll=UNROLL_AB_SAMPLE)
    lp = _chunk_len(tp, ML_CHUNK)
    ls = _chunk_len(ts, ML_CHUNK)
    mixed, mc_p, mn_p, mm_p = _mlstm(proj, gates, bias, ml_w, mixed, row0=0, batch=bp, seq=tp, L=lp,
                                     rows_per_step=8 * lp, unroll=UNROLL_AB_PROMPT)
    m0 =jnp.broadcast_to(state_mlstm_m[0][:, :, None, None], (bs, ML_HEADS, 1, GATE_LANES))
    mixed, mc_s, mn_s, mm_s = _mlstm(proj, gates, bias, ml_w, mixed, row0=mp, batch=bs, seq=ts, L=ls,
                                     rows_per_step=8 * ls, c0=state_mlstm_C[0],
                                     n0=state_mlstm_n[0][:, :, None, :], m0=m0, unroll=UNROLL_AB_SAMPLE)
    x, xg, ssq = _matmul_residual(mixed, w_out_ab_b, 0, x, TM_OUT, 512, next_norm_w=norm_ffn_w[0],
                                  name="out_proj_ab")
    x, xg, ssq = _swiglu_residual(x, xg, ssq, w_gate_b, w_up_b, w_down_b, 0, norm_mix_w[1])

    proj = _matmul(xg, w_in_c_b, 0, tm, 512, scale=_rms_scale(ssq, d), name="in_proj_c")
    mixed = jax.ShapeDtypeStruct((mp + ms, d), BF16)
    hg_w =hg_norm_w[0].astype(F32).reshape(1, -1)
    lbl = lb_logits.astype(F32)
    lp = _chunk_len(tp, HG_CHUNK)
    ls = _chunk_len(ts, HG_CHUNK)
    mixed, hg_p = _hgrn(proj, lbl, hg_w, mixed, row0=0, batch=bp, seq=tp, L=lp, rows_per_step=32 * lp,
                        unroll=UNROLL_HG_PROMPT)
    mixed, hg_s = _hgrn(proj, lbl, hg_w, mixed, row0=mp, batch=bs, seq=ts, L=ls, rows_per_step=32 * ls,
                        s0=state_hgrn[0], unroll=UNROLL_HG_SAMPLE)
    x, xg, ssq = _matmul_residual(mixed, w_out_c_b, 0, x, TM_OUT, 512, next_norm_w=norm_ffn_w[1],
                                  name="out_proj_c")
    x = _swiglu_residual(x, xg, ssq, w_gate_b, w_up_b, w_down_b, 1, None)

    y_prompt = _rmsnorm(x, norm_final_w, F32, row0=0, rows=mp).reshape(bp, tp, d)
    y_sample = _rmsnorm(x, norm_final_w, F32, row0=mp, rows=ms).reshape(bs, ts, d)
    return (y_prompt, y_sample,
            ret_p[None], mc_p[None], mn_p[:, :, 0, :][None], mm_p[:, :, 0, 0][None], hg_p[None],
            ret_s[None], mc_s[None], mn_s[:, :, 0, :][None], mm_s[:, :, 0, 0][None], hg_s[None])
```

```python
import functools

import numpy as np
import jax
import jax.numpy as jnp
from jax import lax
from jax.experimental import pallas as pl
from jax.experimental.pallas import tpu as pltpu

F32 = jnp.float32
BF16 = jnp.bfloat16

EPS = 1e-6
ROPE_BASE = 10000.0
PAST_LEN = 16384

RET_HEADS = 8
ML_HEADS = 8
HEAD_DIM_AB = 256
HG_DIM = 128
RET_CHUNK = 128
ML_CHUNK = 128
HG_CHUNK = 32
LANES = 128
GATE_LANES = LANES

V7X_VMEM_LIMIT_BYTES = 56 * 1024 * 1024

TM_WIDE = 1536
TM_GATE_UP = 3072
GATE_UP_ROW_CHUNKS = 12
TM_OUT = 1024
TM_DOWN = 1024

UNROLL_AB_PROMPT = 2
UNROLL_AB_SAMPLE = 8
UNROLL_HG_PROMPT = 32
UNROLL_HG_SAMPLE = 8


def _cparams(semantics, **kwargs):
    return pltpu.CompilerParams(dimension_semantics=semantics,
                                vmem_limit_bytes=V7X_VMEM_LIMIT_BYTES, **kwargs)


def _recurrence_cparams():
    return _cparams(("parallel", "parallel", "arbitrary"))


def _rmsnorm_kernel(x_ref, w_ref, o_ref):
    x = x_ref[...]
    y = x * lax.rsqrt(jnp.mean(x * x, axis=-1, keepdims=True) + EPS)
    o_ref[...] = (y * w_ref[...]).astype(o_ref.dtype)


def _rmsnorm(x, w, out_dtype, tm=512, row0=0, rows=None):
    d = x.shape[1]
    rows = x.shape[0] if rows is None else rows
    rb0 = row0 // tm
    return pl.pallas_call(
        _rmsnorm_kernel,
        grid=(rows // tm,),
        in_specs=[pl.BlockSpec((tm, d), lambda i: (rb0 + i, 0)),
                  pl.BlockSpec((1, d), lambda i: (0, 0))],
        out_specs=pl.BlockSpec((tm, d), lambda i: (i, 0)),
        out_shape=jax.ShapeDtypeStruct((rows, d), out_dtype),
        compiler_params=_cparams(("parallel",)),
        name="rmsnorm",
    )(x, w.reshape(1, d).astype(F32))


def _rmsnorm_stack_kernel(x_ref, w_ref, *rest):
    xo_ref, ho_ref = rest[-2:]
    x = x_ref[...]
    xo_ref[...] = x
    y = x * lax.rsqrt(jnp.mean(x * x, axis=-1, keepdims=True) + EPS)
    ho_ref[...] = (y * w_ref[...]).astype(ho_ref.dtype)


def _rmsnorm_stack(parts, w, tm=512):
    d = parts[0].shape[1]
    m = sum(p.shape[0] for p in parts)
    out_shape = [jax.ShapeDtypeStruct((m, d), F32), jax.ShapeDtypeStruct((m, d), BF16)]
    w2 = w.reshape(1, d).astype(F32)
    outs = None
    row0 = 0
    for part in parts:
        rb0 = row0 // tm
        out_spec = pl.BlockSpec((tm, d), lambda i, rb0=rb0: (rb0 + i, 0))
        in_specs = [pl.BlockSpec((tm, d), lambda i: (i, 0)), pl.BlockSpec((1, d), lambda i: (0, 0))]
        args = [part, w2]
        aliases = {}
        if outs is not None:
            in_specs += [pl.BlockSpec(memory_space=pl.ANY)] * 2
            args += list(outs)
            aliases = {2: 0, 3: 1}
        outs = pl.pallas_call(
            _rmsnorm_stack_kernel,
            grid=(part.shape[0] // tm,),
            in_specs=in_specs,
            out_specs=[out_spec, out_spec],
            out_shape=out_shape,
            input_output_aliases=aliases,
            compiler_params=_cparams(("parallel",)),
            name="rmsnorm_stack",
        )(*args)
        row0 += part.shape[0]
    return outs


def _mm_kernel(x_ref, w_ref, o_ref):
    w = w_ref[...].astype(BF16)
    o_ref[...] = jnp.dot(x_ref[...], w, preferred_element_type=F32).astype(o_ref.dtype)


def _mm_residual_kernel(x_ref, w_ref, r_ref, o_ref):
    w = w_ref[...].astype(BF16)
    o_ref[...] = r_ref[...] + jnp.dot(x_ref[...], w, preferred_element_type=F32)


def _gate_up_kernel(x_ref, wg_ref, wu_ref, o_ref):
    x = x_ref[...]
    a = jnp.dot(x, wg_ref[...].astype(BF16), preferred_element_type=F32)
    b = jnp.dot(x, wu_ref[...].astype(BF16), preferred_element_type=F32)
    o_ref[...] = (jax.nn.silu(a) * b).astype(o_ref.dtype)


def _mm_residual_prenorm_kernel(x_ref, w_ref, r_ref, g_ref, o_ref, yg_ref, ssq_ref):
    w = w_ref[...].astype(BF16)
    y = r_ref[...] + jnp.dot(x_ref[...], w, preferred_element_type=F32)
    o_ref[...] = y
    yg_ref[...] = (y * g_ref[...]).astype(yg_ref.dtype)
    sq = y * y
    part = sq[:, :LANES]
    for c in range(1, sq.shape[1] // LANES):
        part = part + sq[:, c * LANES:(c + 1) * LANES]

    @pl.when(pl.program_id(1) == 0)
    def _():
        ssq_ref[...] = part

    @pl.when(pl.program_id(1) > 0)
    def _():
        ssq_ref[...] += part


def _rms_scale_kernel(d_model, ssq_ref, o_ref):
    ms = jnp.sum(ssq_ref[...], axis=-1, keepdims=True) / d_model
    o_ref[...] = jnp.broadcast_to(lax.rsqrt(ms + EPS), o_ref.shape)


def _rms_scale(ssq, d_model, tm=1024):
    m = ssq.shape[0]
    spec = pl.BlockSpec((tm, LANES), lambda i: (i, 0))
    return pl.pallas_call(
        functools.partial(_rms_scale_kernel, d_model),
        grid=(m // tm,),
        in_specs=[spec],
        out_specs=spec,
        out_shape=jax.ShapeDtypeStruct((m, LANES), F32),
        compiler_params=_cparams(("parallel",)),
        name="rms_scale",
    )(ssq)


def _lane_tiled(scale_ref, width):
    r = scale_ref[...]
    return jnp.concatenate([r] * (width // LANES), axis=1)


def _mm_postnorm_kernel(x_ref, w_ref, scale_ref, o_ref):
    r = _lane_tiled(scale_ref, o_ref.shape[1])
    w = w_ref[...].astype(BF16)
    acc = jnp.dot(x_ref[...], w, preferred_element_type=F32)
    o_ref[...] = (r * acc).astype(o_ref.dtype)


def _gate_up_postnorm_kernel(x_ref, wg_ref, wu_ref, scale_ref, o_ref):
    tm, tn = o_ref.shape
    wg = wg_ref[...].astype(BF16)
    wu = wu_ref[...].astype(BF16)
    rc = tm // GATE_UP_ROW_CHUNKS
    for c in range(GATE_UP_ROW_CHUNKS):
        rows = pl.ds(c * rc, rc)
        r = _lane_tiled(scale_ref.at[rows, :], tn)
        x = x_ref[rows, :]
        a = r * jnp.dot(x, wg, preferred_element_type=F32)
        b = r * jnp.dot(x, wu, preferred_element_type=F32)
        o_ref[rows, :] = (jax.nn.silu(a) * b).astype(o_ref.dtype)


def _weight_spec(w, layer, tn, col_block0=0):
    return pl.BlockSpec((None, w.shape[1], tn), lambda i, j: (layer, 0, col_block0 + j))


def _matmul(x, w, layer, tm, tn, n_out=None, col_block0=0, scale=None, out_dtype=F32, name="matmul"):
    m, k = x.shape
    n = w.shape[2] if n_out is None else n_out
    in_specs = [pl.BlockSpec((tm, k), lambda i, j: (i, 0)),
                _weight_spec(w, layer, tn, col_block0)]
    args = [x, w]
    body = _mm_kernel
    if scale is not None:
        in_specs.append(pl.BlockSpec((tm, LANES), lambda i, j: (i, 0)))
        args.append(scale)
        body = _mm_postnorm_kernel
    return pl.pallas_call(
        body,
        grid=(m // tm, n // tn),
        in_specs=in_specs,
        out_specs=pl.BlockSpec((tm, tn), lambda i, j: (i, j)),
        out_shape=jax.ShapeDtypeStruct((m, n), out_dtype),
        compiler_params=_cparams(("parallel", "arbitrary")),
        name=name,
    )(*args)


def _matmul_residual(x, w, layer, res, tm, tn, single_buffer_lhs=False, next_norm_w=None,
                     name="matmul_residual"):
    m, k = x.shape
    n = w.shape[2]
    lhs_kwargs = {"pipeline_mode": pl.Buffered(1)} if single_buffer_lhs else {}
    tile = pl.BlockSpec((tm, tn), lambda i, j: (i, j))
    in_specs = [pl.BlockSpec((tm, k), lambda i, j: (i, 0), **lhs_kwargs),
                _weight_spec(w, layer, tn), tile]
    args = [x, w, res]
    if next_norm_w is None:
        body, out_specs, out_shape = _mm_residual_kernel, tile, jax.ShapeDtypeStruct((m, n), F32)
    else:
        in_specs.append(pl.BlockSpec((1, tn), lambda i, j: (0, j)))
        args.append(next_norm_w.reshape(1, n).astype(F32))
        body = _mm_residual_prenorm_kernel
        out_specs = [tile, tile, pl.BlockSpec((tm, LANES), lambda i, j: (i, 0))]
        out_shape = [jax.ShapeDtypeStruct((m, n), F32), jax.ShapeDtypeStruct((m, n), BF16),
                     jax.ShapeDtypeStruct((m, LANES), F32)]
    return pl.pallas_call(
        body,
        grid=(m // tm, n // tn),
        in_specs=in_specs,
        out_specs=out_specs,
        out_shape=out_shape,
        compiler_params=_cparams(("parallel", "arbitrary")),
        name=name,
    )(*args)


def _gate_up(x, wg, wu, layer, tm, tn, scale):
    m, k = x.shape
    n = wg.shape[2]
    return pl.pallas_call(
        _gate_up_postnorm_kernel,
        grid=(m // tm, n // tn),
        in_specs=[pl.BlockSpec((tm, k), lambda i, j: (i, 0), pipeline_mode=pl.Buffered(1)),
                  _weight_spec(wg, layer, tn),
                  _weight_spec(wu, layer, tn),
                  pl.BlockSpec((tm, LANES), lambda i, j: (i, 0))],
        out_specs=pl.BlockSpec((tm, tn), lambda i, j: (i, j)),
        out_shape=jax.ShapeDtypeStruct((m, n), BF16),
        compiler_params=_cparams(("parallel", "arbitrary")),
        name="gate_up",
    )(x, wg, wu, scale)


def _dot_nt(a, b):
    return lax.dot_general(a, b, (((1,), (1,)), ((), ())), preferred_element_type=F32)


def _dot_tn(a, b):
    return lax.dot_general(a, b, (((0,), (0,)), ((), ())), preferred_element_type=F32)


def _dot(a, b):
    return jnp.dot(a, b, preferred_element_type=F32)


def _head_norm(x):
    return x * lax.rsqrt(jnp.mean(x * x, axis=-1, keepdims=True) + EPS)


def _col_to_row(col, eye):
    return jnp.sum(jnp.where(eye, col, 0.0), axis=0, keepdims=True)


def _row_to_col(row, eye):
    return jnp.sum(jnp.where(eye, row, 0.0), axis=1, keepdims=True)


def _alias_existing(mixed, in_specs, args):
    if isinstance(mixed, jax.ShapeDtypeStruct):
        return {}
    in_specs.append(pl.BlockSpec(memory_space=pl.ANY))
    args.append(mixed)
    return {len(args) - 1: 0}


def _rotary(x, cos, sin):
    half = x.shape[-1] // 2
    x1, x2 = x[:, :half], x[:, half:]
    return jnp.concatenate([x1 * cos - x2 * sin, x1 * sin + x2 * cos], axis=-1)


def _ret_kernel(L, units, carry, unroll, *refs):
    (q_ref, k_ref, v_ref, g_ref, cos_ref, sin_ref, dec_ref, qd_ref, kd_ref, cd_ref) = refs[:10]
    s0_ref = None if carry else refs[10]
    o_ref, sout_ref = refs[-2:]

    if carry:
        @pl.when(pl.program_id(2) == 0)
        def _():
            sout_ref[...] = jnp.zeros_like(sout_ref)

    dec = dec_ref[...]
    qd = qd_ref[...]
    kd = kd_ref[...]
    cd = cd_ref[...]
    scale = HEAD_DIM_AB ** -0.5

    def unit(u, c):
        rows = pl.ds(pl.multiple_of(u * L, L), L)
        if carry:
            cos, sin = cos_ref[rows, :], sin_ref[rows, :]
            s = sout_ref[...]
        else:
            cos, sin = cos_ref[...], sin_ref[...]
            s = s0_ref[u]
        q = _rotary(q_ref[rows, :], cos, sin)
        k = _rotary(k_ref[rows, :], cos, sin) * scale
        v = v_ref[rows, :]
        att = _dot_nt(q, k) * dec
        o = _dot(att, v) + _dot(q, s) * qd
        s_new = cd * s + _dot_tn(k * kd, v)
        out = jax.nn.silu(g_ref[rows, :]) * _head_norm(o)
        o_ref[rows, :] = out.astype(o_ref.dtype)
        if carry:
            sout_ref[...] = s_new
        else:
            sout_ref[u] = s_new
        return c

    lax.fori_loop(0, units, unit, 0, unroll=unroll)


def _retention_tables(L):
    log_gamma = jnp.log1p(-jnp.exp2(-5.0 - jnp.arange(RET_HEADS, dtype=F32)))[:, None]
    idx = jnp.arange(L, dtype=F32)
    diff = idx[:, None] - idx[None, :]
    decay = jnp.exp(jnp.where(diff >= 0, log_gamma[:, :, None] * diff, -jnp.inf))
    q_decay = jnp.exp(log_gamma * (idx + 1.0))[:, :, None]
    k_decay = jnp.exp(log_gamma * (L - 1.0 - idx))[:, :, None]
    chunk_decay = jnp.exp(log_gamma[:, 0] * L)[:, None, None]
    return decay, q_decay, k_decay, chunk_decay


def _rope_tables(pos):
    half = HEAD_DIM_AB // 2
    inv_freq = ROPE_BASE ** (-jnp.arange(half, dtype=F32) / half)
    ang = pos[:, None] * inv_freq[None, :]
    return jnp.cos(ang), jnp.sin(ang)


def _retention(proj, mixed, *, row0, batch, seq, L, rows_per_step, pos0, s0=None, unroll=1):
    d = HEAD_DIM_AB
    h_ = RET_HEADS
    carry = s0 is None
    units = rows_per_step // L
    decay, q_decay, k_decay, chunk_decay = _retention_tables(L)
    cos, sin = _rope_tables(pos0 + jnp.arange(seq, dtype=F32))
    rb0 = row0 // rows_per_step
    if carry:
        tblocks = seq // rows_per_step
        grid = (batch, h_, tblocks)
        row_idx = lambda b, h, t: rb0 + b * tblocks + t
        rope_spec = pl.BlockSpec((rows_per_step, d // 2), lambda b, h, t: (t, 0))
        sout_spec = pl.BlockSpec((None, None, d, d), lambda b, h, t: (b, h, 0, 0))
    else:
        nb = rows_per_step // seq
        grid = (batch // nb, h_, 1)
        row_idx = lambda b, h, t: rb0 + b
        rope_spec = pl.BlockSpec((seq, d // 2), lambda b, h, t: (0, 0))
        sout_spec = pl.BlockSpec((nb, None, d, d), lambda b, h, t: (b, h, 0, 0))

    def col_spec(c0):
        return pl.BlockSpec((rows_per_step, d), lambda b, h, t: (row_idx(b, h, t), c0 + h))

    in_specs = [col_spec(0), col_spec(h_), col_spec(2 * h_), col_spec(3 * h_),
                rope_spec, rope_spec,
                pl.BlockSpec((None, L, L), lambda b, h, t: (h, 0, 0)),
                pl.BlockSpec((None, L, 1), lambda b, h, t: (h, 0, 0)),
                pl.BlockSpec((None, L, 1), lambda b, h, t: (h, 0, 0)),
                pl.BlockSpec((None, 1, 1), lambda b, h, t: (h, 0, 0))]
    args = [proj, proj, proj, proj, cos, sin, decay, q_decay, k_decay, chunk_decay]
    if not carry:
        in_specs.append(sout_spec)
        args.append(s0)
    aliases = _alias_existing(mixed, in_specs, args)
    return pl.pallas_call(
        functools.partial(_ret_kernel, L, units, carry, unroll),
        grid=grid,
        in_specs=in_specs,
        out_specs=[col_spec(0), sout_spec],
        out_shape=[jax.ShapeDtypeStruct(mixed.shape, mixed.dtype),
                   jax.ShapeDtypeStruct((batch, h_, d, d), F32)],
        input_output_aliases=aliases,
        compiler_params=_recurrence_cparams(),
        name="retention_prompt" if carry else "retention_sample",
    )(*args)


def _mlstm_kernel(L, units, carry, unroll, *refs):
    (q_ref, k_ref, v_ref, og_ref, gates_ref, bias_ref, nw_ref) = refs[:7]
    c0_ref, n0_ref, m0_ref = (None, None, None) if carry else refs[7:10]
    o_ref, c_ref, n_ref, m_ref = refs[-4:]

    if carry:
        @pl.when(pl.program_id(2) == 0)
        def _():
            c_ref[...] = jnp.zeros_like(c_ref)
            n_ref[...] = jnp.zeros_like(n_ref)
            m_ref[...] = jnp.zeros_like(m_ref)

    head = pl.program_id(1)
    bias = bias_ref[...]
    nw = nw_ref[...]
    lane = lax.broadcasted_iota(jnp.int32, (L, GATE_LANES), 1)
    ti = lax.broadcasted_iota(jnp.int32, (L, L), 0)
    si = lax.broadcasted_iota(jnp.int32, (L, L), 1)
    eye = ti == si
    causal = si <= ti
    scale = HEAD_DIM_AB ** -0.5

    def gate_terms(rows, m):
        x = gates_ref[rows, :] + bias
        ig = jnp.sum(jnp.where(lane == head, x, 0.0), axis=1, keepdims=True)
        fpre = jnp.sum(jnp.where(lane == head + ML_HEADS, x, 0.0), axis=1, keepdims=True)
        lf = jnp.minimum(fpre, 0.0) - jnp.log1p(jnp.exp(-jnp.abs(fpre)))
        f_cum = jnp.sum(jnp.where(causal, _col_to_row(lf, eye), 0.0), axis=1, keepdims=True)
        a = ig - f_cum
        a_row = _col_to_row(a, eye)
        cmax = jnp.max(jnp.where(causal, a_row, -jnp.inf), axis=1, keepdims=True)
        m_t = f_cum + jnp.maximum(m, cmax)
        f_end = f_cum[L - 1:L, :]
        m_end = m_t[L - 1:L, :]
        return dict(
            d_mat=jnp.exp(jnp.where(causal, (f_cum - m_t) + a_row, -jnp.inf)),
            inter=jnp.exp(f_cum + m - m_t),
            floor=jnp.exp(-m_t),
            w=jnp.exp(f_end - m_end + a),
            cdec=jnp.exp(f_end + m - m_end),
            m_end=m_end)

    def gate_free_products(rows, c):
        q = q_ref[rows, :]
        return _dot_nt(q, k_ref[rows, :] * scale), (None if c is None else _dot(q, c))

    def mix(rows, g, c, n, products):
        q = q_ref[rows, :]
        k = k_ref[rows, :] * scale
        v = v_ref[rows, :]
        qk, qc = products
        if qc is None:
            qc = _dot(q, c)
        s = qk * g["d_mat"]
        num = g["inter"] * qc + _dot(s, v)
        den = (g["inter"] * jnp.sum(q * n, axis=1, keepdims=True)
               + jnp.sum(s, axis=1, keepdims=True))
        hh = num * (1.0 / jnp.maximum(jnp.abs(den), g["floor"]))
        kw = k * g["w"]
        c_new = g["cdec"] * c + _dot_tn(kw, v)
        n_new = g["cdec"] * n + jnp.sum(kw, axis=0, keepdims=True)
        out = jax.nn.sigmoid(og_ref[rows, :]) * (_head_norm(hh) * nw)
        o_ref[rows, :] = out.astype(o_ref.dtype)
        return c_new, n_new

    if carry:
        def group(i, m):
            chunks = []
            for j in range(unroll):
                rows = pl.ds(pl.multiple_of((i * unroll + j) * L, L), L)
                g = gate_terms(rows, m)
                m = g["m_end"]
                chunks.append((rows, g))
            for rows, g in chunks:
                c_ref[...], n_ref[...] = mix(rows, g, c_ref[...], n_ref[...],
                                             gate_free_products(rows, None))
            return m

        m_last = lax.fori_loop(0, units // unroll, group, m_ref[:, 0:1])
        m_ref[...] = jnp.broadcast_to(m_last, m_ref.shape)
    else:
        early =[gate_free_products(pl.ds(u * L, L), c0_ref[u]) for u in range(units)]
        terms = [gate_terms(pl.ds(u * L, L), m0_ref[u][:, 0:1]) for u in range(units)]
        for u in range(units):
            rows = pl.ds(u * L, L)
            g = terms[u]
            c_ref[u], n_ref[u] = mix(rows, g, c0_ref[u], n0_ref[u], early[u])
            m_ref[u] = jnp.broadcast_to(g["m_end"], (1, GATE_LANES))


def _mlstm(proj, gates, bias, norm_w, mixed, *, row0, batch, seq, L, rows_per_step,
           c0=None, n0=None, m0=None, unroll=1):
    d = HEAD_DIM_AB
    h_ = ML_HEADS
    carry = c0 is None
    units = rows_per_step // L
    rb0 = row0 // rows_per_step
    col0 = 4 * RET_HEADS
    if carry:
        tblocks = seq // rows_per_step
        grid = (batch, h_, tblocks)
        row_idx = lambda b, h, t: rb0 + b * tblocks + t
        lead = None
    else:
        nb = rows_per_step // seq
        grid = (batch // nb, h_, 1)
        row_idx = lambda b, h, t: rb0 + b
        lead = nb
    c_spec = pl.BlockSpec((lead, None, d, d), lambda b, h, t: (b, h, 0, 0))
    n_spec = pl.BlockSpec((lead, None, 1, d), lambda b, h, t: (b, h, 0, 0))
    m_spec = pl.BlockSpec((lead, None, 1, GATE_LANES), lambda b, h, t: (b, h, 0, 0))

    def col_spec(c_blk):
        return pl.BlockSpec((rows_per_step, d), lambda b, h, t: (row_idx(b, h, t), c_blk + h))

    in_specs = [col_spec(col0), col_spec(col0 + h_), col_spec(col0 + 2 * h_), col_spec(col0 + 3 * h_),
                pl.BlockSpec((rows_per_step, GATE_LANES), lambda b, h, t: (row_idx(b, h, t), 0)),
                pl.BlockSpec((1, GATE_LANES), lambda b, h, t: (0, 0)),
                pl.BlockSpec((1, d), lambda b, h, t: (0, h))]
    args = [proj, proj, proj, proj, gates, bias, norm_w]
    if not carry:
        in_specs += [c_spec, n_spec, m_spec]
        args += [c0, n0, m0]
    aliases = _alias_existing(mixed, in_specs, args)
    return pl.pallas_call(
        functools.partial(_mlstm_kernel, L, units, carry, unroll),
        grid=grid,
        in_specs=in_specs,
        out_specs=[col_spec(RET_HEADS), c_spec, n_spec, m_spec],
        out_shape=[jax.ShapeDtypeStruct(mixed.shape, mixed.dtype),
                   jax.ShapeDtypeStruct((batch, h_, d, d), F32),
                   jax.ShapeDtypeStruct((batch, h_, 1, d), F32),
                   jax.ShapeDtypeStruct((batch, h_, 1, GATE_LANES), F32)],
        input_output_aliases=aliases,
        compiler_params=_recurrence_cparams(),
        name="mlstm_prompt" if carry else "mlstm_sample",
    )(*args)


def _hgrn_kernel(L, units, carry, unroll, *refs):
    (q_ref, f_ref, i_ref, g_ref, lbl_ref, nw_ref) = refs[:6]
    rows_ref = refs[-1]
    if carry:
        s0_ref = None
        o_ref, sout_ref, st_ref = refs[-4:-1]
    else:
        s0_ref = refs[6]
        o_ref, sout_ref, kd_ref, eg_ref = refs[-5:-1]
        st_ref = None

    if carry:
        @pl.when(pl.program_id(2) == 0)
        def _():
            st_ref[...] = jnp.zeros_like(st_ref)

    d = HG_DIM
    lg = lbl_ref[...]
    e = jnp.exp(lg - jnp.max(lg, axis=0, keepdims=True))
    sm = e / jnp.sum(e, axis=0, keepdims=True)
    lb = (sm[0:1, :] + sm[1:2, :]) - sm[0:1, :]
    nw = nw_ref[...]
    groups = L // 8
    pairs = [(j, jp) for jp in range(groups - 1) for j in range(jp + 1, groups)]
    row8 = lax.broadcasted_iota(jnp.int32, (8, d), 0)
    lane = lax.broadcasted_iota(jnp.int32, (8, L), 1)
    if not carry:
        eye_d = (lax.broadcasted_iota(jnp.int32, (d, d), 0)
                 == lax.broadcasted_iota(jnp.int32, (d, d), 1))

    def unit(u, carry_val):
        rows = pl.ds(pl.multiple_of(u * L, L), L)
        q = jax.nn.silu(q_ref[rows, :])
        fg = lb + (1.0 - lb) * jax.nn.sigmoid(f_ref[rows, :])
        k = 1.0 - fg
        logf = jnp.log(fg)
        v = i_ref[rows, :]
        gg = []
        base = jnp.zeros((1, d), F32)
        for j in range(groups):
            cum = logf[8 * j:8 * j + 8, :]
            for sh in (1, 2, 4):
                cum = cum + jnp.where(row8 >= sh, pltpu.roll(cum, sh, 0), 0.0)
            cum = cum + base
            gg.append(cum)
            base = cum[7:8, :]
        g_end = base
        g_cum = jnp.concatenate(gg, axis=0)
        qg = [q[8 * j:8 * j + 8, :] for j in range(groups)]
        kg = [k[8 * j:8 * j + 8, :] for j in range(groups)]
        rows_ref[0, rows, :] = g_cum
        rows_ref[1, rows, :] = k
        row0 = pl.multiple_of(u * L, L)
        att = []
        for j in range(groups):
            a = jnp.zeros((8, L), F32)
            for r in range(8):
                g_r = rows_ref[0, pl.ds(row0 + 8 * j + r, 1), :]
                k_r = rows_ref[1, pl.ds(row0 + 8 * j + r, 1), :]
                diff = jnp.where(row8 >= r, gg[j] - g_r, -jnp.inf)
                col = jnp.sum(qg[j] * k_r * jnp.exp(diff), axis=1, keepdims=True)
                a = jnp.where(lane == 8 * j + r, col, a)
            att.append(a)
        if pairs:
            ends = [gg[j][7:8, :] for j in range(groups)]
            kt = jnp.concatenate([kg[j] * jnp.exp(ends[j] - gg[j]) for j in range(groups)], axis=0)
            qt = jnp.concatenate([qg[j] * jnp.exp(gg[j] - ends[jp]) for j, jp in pairs], axis=0)
            cross = _dot_nt(qt, kt)
            for idx, (j, jp) in enumerate(pairs):
                in_group = (lane >= 8 * jp) & (lane < 8 * jp + 8)
                att[j] = jnp.where(in_group, cross[8 * idx:8 * idx + 8, :], att[j])
        att_m = jnp.concatenate(att, axis=0)
        qs = q * jnp.exp(g_cum)
        kd = k * jnp.exp(g_end - g_cum)
        if carry:
            st = st_ref[...]
            o = _dot_nt(qs, st) + _dot(att_m, v)
            st_ref[...] = st * jnp.exp(g_end) + _dot_tn(v, kd)
        else:
            o = _dot(qs, s0_ref[u]) + _dot(att_m, v)
            kd_ref[rows, :] = kd
            eg_ref[pl.ds(u, 1), :] = jnp.exp(g_end)
        out = (_head_norm(o) * nw) * jax.nn.silu(g_ref[rows, :])
        o_ref[rows, :] = out.astype(o_ref.dtype)
        return carry_val

    lax.fori_loop(0, units, unit, 0, unroll=unroll)

    if carry:
        @pl.when(pl.program_id(2) == pl.num_programs(2) - 1)
        def _():
            sout_ref[...] = st_ref[...].T
    else:
        per_tile = d // L
        row_chunk = lax.broadcasted_iota(jnp.int32, (d, d), 0) // L
        for t in range(units // per_tile):
            kd_t = kd_ref[t * d:(t + 1) * d, :].T
            v_t = i_ref[t * d:(t + 1) * d, :]
            for c in range(per_tile):
                u = t * per_tile + c
                upd = _dot(kd_t, jnp.where(row_chunk == c, v_t, 0.0))
                sout_ref[u] = _row_to_col(eg_ref[u:u + 1, :], eye_d) * s0_ref[u] + upd


def _hgrn(proj, lb_logits, norm_w, mixed, *, row0, batch, seq, L, rows_per_step, s0=None, unroll=1):
    d = HG_DIM
    h_ = proj.shape[1] // (4 * d)
    carry = s0 is None
    units = rows_per_step // L
    rb0 = row0 // rows_per_step
    if carry:
        tblocks = seq // rows_per_step
        grid = (batch, h_, tblocks)
        row_idx = lambda b, h, t: rb0 + b * tblocks + t
        sout_spec = pl.BlockSpec((None, None, d, d), lambda b, h, t: (b, h, 0, 0))
        scratch = [pltpu.VMEM((d, d), F32), pltpu.VMEM((2, rows_per_step, d), F32)]
    else:
        nb = rows_per_step // seq
        grid = (batch // nb, h_, 1)
        row_idx = lambda b, h, t: rb0 + b
        sout_spec = pl.BlockSpec((nb, None, d, d), lambda b, h, t: (b, h, 0, 0))
        scratch = [pltpu.VMEM((rows_per_step, d), F32), pltpu.VMEM((units, d), F32),
                   pltpu.VMEM((2, rows_per_step, d), F32)]

    def col_spec(c0):
        return pl.BlockSpec((rows_per_step, d), lambda b, h, t: (row_idx(b, h, t), c0 + h))

    in_specs = [col_spec(0), col_spec(h_), col_spec(2 * h_), col_spec(3 * h_),
                pl.BlockSpec((2, d), lambda b, h, t: (0, h)),
                pl.BlockSpec((1, d), lambda b, h, t: (0, 0))]
    args = [proj, proj, proj, proj, lb_logits, norm_w]
    if not carry:
        in_specs.append(sout_spec)
        args.append(s0)
    aliases = _alias_existing(mixed, in_specs, args)
    return pl.pallas_call(
        functools.partial(_hgrn_kernel, L, units, carry, unroll),
        grid=grid,
        in_specs=in_specs,
        out_specs=[col_spec(0), sout_spec],
        out_shape=[jax.ShapeDtypeStruct(mixed.shape, mixed.dtype),
                   jax.ShapeDtypeStruct((batch, h_, d, d), F32)],
        scratch_shapes=scratch,
        input_output_aliases=aliases,
        compiler_params=_recurrence_cparams(),
        name="hgrn_prompt" if carry else "hgrn_sample",
    )(*args)


def _chunk_len(t, c):
    return c if t % c == 0 else t


def _swiglu_residual(x, xg, ssq, w_gate, w_up, w_down, layer, next_norm_w):
    act = _gate_up(xg, w_gate, w_up, layer, TM_GATE_UP, 256, _rms_scale(ssq, x.shape[1]))
    return _matmul_residual(act, w_down, layer, x, TM_DOWN, 256, single_buffer_lhs=True,
                            next_norm_w=next_norm_w, name="down_proj")


def kernel(x_prompt, x_sample, state_ret, state_mlstm_C, state_mlstm_n, state_mlstm_m, state_hgrn,
           norm_mix_w, w_in_ab, b_if_ab, ml_norm_w, w_out_ab, w_in_c, lb_logits, hg_norm_w, w_out_c,
           norm_ffn_w, w_gate, w_up, w_down, norm_final_w):
    bp, tp, d = x_prompt.shape
    bs, ts, _ = x_sample.shape
    mp, ms = bp * tp, bs * ts
    tm = TM_WIDE

    w_in_ab_b, w_out_ab_b = w_in_ab.astype(BF16), w_out_ab
    w_in_c_b, w_out_c_b = w_in_c, w_out_c
    w_gate_b, w_up_b, w_down_b = w_gate, w_up, w_down

    ab_main = 4 * RET_HEADS * HEAD_DIM_AB + 4 * ML_HEADS * HEAD_DIM_AB
    x, hn = _rmsnorm_stack([x_prompt.reshape(mp, d), x_sample.reshape(ms, d)], norm_mix_w[0])
    proj = _matmul(hn, w_in_ab_b, 0, tm, 512, n_out=ab_main, name="in_proj_ab")
    gates = _matmul(hn, w_in_ab_b, 0, tm, GATE_LANES, n_out=GATE_LANES,
                    col_block0=ab_main // GATE_LANES, name="in_proj_gates")
    bias = jnp.pad(b_if_ab[0].astype(F32), (0, GATE_LANES - 2 * ML_HEADS)).reshape(1, GATE_LANES)
    ml_w = ml_norm_w[0].astype(F32).reshape(1, -1)

    mixed = jax.ShapeDtypeStruct((mp + ms, d), BF16)
    lp = _chunk_len(tp, RET_CHUNK)
    ls = _chunk_len(ts, RET_CHUNK)
    mixed, ret_p = _retention(proj, mixed, row0=0, batch=bp, seq=tp, L=lp, rows_per_step=8 * lp,
                              pos0=0.0, unroll=UNROLL_AB_PROMPT)
    mixed, ret_s = _retention(proj, mixed, row0=mp, batch=bs, seq=ts, L=ls, rows_per_step=8 * ls,
                              pos0=float(PAST_LEN), s0=state_ret[0], unroll=UNROLL_AB_SAMPLE)
    lp = _chunk_len(tp, ML_CHUNK)
    ls = _chunk_len(ts, ML_CHUNK)
    mixed, mc_p, mn_p, mm_p = _mlstm(proj, gates, bias, ml_w, mixed, row0=0, batch=bp, seq=tp, L=lp,
                                     rows_per_step=8 * lp, unroll=UNROLL_AB_PROMPT)
    m0 =jnp.broadcast_to(state_mlstm_m[0][:, :, None, None], (bs, ML_HEADS, 1, GATE_LANES))
    mixed, mc_s, mn_s, mm_s = _mlstm(proj, gates, bias, ml_w, mixed, row0=mp, batch=bs, seq=ts, L=ls,
                                     rows_per_step=8 * ls, c0=state_mlstm_C[0],
                                     n0=state_mlstm_n[0][:, :, None, :], m0=m0, unroll=UNROLL_AB_SAMPLE)
    x, xg, ssq = _matmul_residual(mixed, w_out_ab_b, 0, x, TM_OUT, 512, next_norm_w=norm_ffn_w[0],
                                  name="out_proj_ab")
    x, xg, ssq = _swiglu_residual(x, xg, ssq, w_gate_b, w_up_b, w_down_b, 0, norm_mix_w[1])

    proj = _matmul(xg, w_in_c_b, 0, tm, 512, scale=_rms_scale(ssq, d), name="in_proj_c")
    mixed = jax.ShapeDtypeStruct((mp + ms, d), BF16)
    hg_w =hg_norm_w[0].astype(F32).reshape(1, -1)
    lbl = lb_logits.astype(F32)
    lp = _chunk_len(tp, HG_CHUNK)
    ls = _chunk_len(ts, HG_CHUNK)
    mixed, hg_p = _hgrn(proj, lbl, hg_w, mixed, row0=0, batch=bp, seq=tp, L=lp, rows_per_step=32 * lp,
                        unroll=UNROLL_HG_PROMPT)
    mixed, hg_s = _hgrn(proj, lbl, hg_w, mixed, row0=mp, batch=bs, seq=ts, L=ls, rows_per_step=32 * ls,
                        s0=state_hgrn[0], unroll=UNROLL_HG_SAMPLE)
    x, xg, ssq = _matmul_residual(mixed, w_out_c_b, 0, x, TM_OUT, 512, next_norm_w=norm_ffn_w[1],
                                  name="out_proj_c")
    x = _swiglu_residual(x, xg, ssq, w_gate_b, w_up_b, w_down_b, 1, None)

    y_prompt = _rmsnorm(x, norm_final_w, F32, row0=0, rows=mp).reshape(bp, tp, d)
    y_sample = _rmsnorm(x, norm_final_w, F32, row0=mp, rows=ms).reshape(bs, ts, d)
    return (y_prompt, y_sample,
            ret_p[None], mc_p[None], mn_p[:, :, 0, :][None], mm_p[:, :, 0, 0][None], hg_p[None],
            ret_s[None], mc_s[None], mn_s[:, :, 0, :][None], mm_s[:, :, 0, 0][None], hg_s[None])
```

```python
import functools

import numpy as np
import jax
import jax.numpy as jnp
from jax import lax
from jax.experimental import pallas as pl
from jax.experimental.pallas import tpu as pltpu

F32 = jnp.float32
BF16 = jnp.bfloat16

EPS = 1e-6
ROPE_BASE = 10000.0
PAST_LEN = 16384

RET_HEADS = 8
ML_HEADS = 8
HEAD_DIM_AB = 256
HG_DIM = 128
RET_CHUNK = 128
ML_CHUNK = 128
HG_CHUNK = 32
LANES = 128
GATE_LANES = LANES

V7X_VMEM_LIMIT_BYTES = 56 * 1024 * 1024

TM_WIDE = 1536
TM_GATE_UP = 3072
GATE_UP_ROW_CHUNKS = 12
TM_OUT = 1024
TM_DOWN = 1024

UNROLL_AB_PROMPT = 2
UNROLL_AB_SAMPLE = 8
UNROLL_HG_PROMPT = 32
UNROLL_HG_SAMPLE = 8


def _cparams(semantics, **kwargs):
    return pltpu.CompilerParams(dimension_semantics=semantics,
                                vmem_limit_bytes=V7X_VMEM_LIMIT_BYTES, **kwargs)


def _recurrence_cparams():
    return _cparams(("parallel", "parallel", "arbitrary"))


def _rmsnorm_kernel(x_ref, w_ref, o_ref):
    x = x_ref[...]
    y = x * lax.rsqrt(jnp.mean(x * x, axis=-1, keepdims=True) + EPS)
    o_ref[...] = (y * w_ref[...]).astype(o_ref.dtype)


def _rmsnorm(x, w, out_dtype, tm=512, row0=0, rows=None):
    d = x.shape[1]
    rows = x.shape[0] if rows is None else rows
    rb0 = row0 // tm
    return pl.pallas_call(
        _rmsnorm_kernel,
        grid=(rows // tm,),
        in_specs=[pl.BlockSpec((tm, d), lambda i: (rb0 + i, 0)),
                  pl.BlockSpec((1, d), lambda i: (0, 0))],
        out_specs=pl.BlockSpec((tm, d), lambda i: (i, 0)),
        out_shape=jax.ShapeDtypeStruct((rows, d), out_dtype),
        compiler_params=_cparams(("parallel",)),
        name="rmsnorm",
    )(x, w.reshape(1, d).astype(F32))


def _rmsnorm_stack_kernel(x_ref, w_ref, *rest):
    xo_ref, ho_ref = rest[-2:]
    x = x_ref[...]
    xo_ref[...] = x
    y = x * lax.rsqrt(jnp.mean(x * x, axis=-1, keepdims=True) + EPS)
    ho_ref[...] = (y * w_ref[...]).astype(ho_ref.dtype)


def _rmsnorm_stack(parts, w, tm=512):
    d = parts[0].shape[1]
    m = sum(p.shape[0] for p in parts)
    out_shape = [jax.ShapeDtypeStruct((m, d), F32), jax.ShapeDtypeStruct((m, d), BF16)]
    w2 = w.reshape(1, d).astype(F32)
    outs = None
    row0 = 0
    for part in parts:
        rb0 = row0 // tm
        out_spec = pl.BlockSpec((tm, d), lambda i, rb0=rb0: (rb0 + i, 0))
        in_specs = [pl.BlockSpec((tm, d), lambda i: (i, 0)), pl.BlockSpec((1, d), lambda i: (0, 0))]
        args = [part, w2]
        aliases = {}
        if outs is not None:
            in_specs += [pl.BlockSpec(memory_space=pl.ANY)] * 2
            args += list(outs)
            aliases = {2: 0, 3: 1}
        outs = pl.pallas_call(
            _rmsnorm_stack_kernel,
            grid=(part.shape[0] // tm,),
            in_specs=in_specs,
            out_specs=[out_spec, out_spec],
            out_shape=out_shape,
            input_output_aliases=aliases,
            compiler_params=_cparams(("parallel",)),
            name="rmsnorm_stack",
        )(*args)
        row0 += part.shape[0]
    return outs


def _mm_kernel(x_ref, w_ref, o_ref):
    w = w_ref[...].astype(BF16)
    o_ref[...] = jnp.dot(x_ref[...], w, preferred_element_type=F32).astype(o_ref.dtype)


def _mm_residual_kernel(x_ref, w_ref, r_ref, o_ref):
    w = w_ref[...].astype(BF16)
    o_ref[...] = r_ref[...] + jnp.dot(x_ref[...], w, preferred_element_type=F32)


def _gate_up_kernel(x_ref, wg_ref, wu_ref, o_ref):
    x = x_ref[...]
    a = jnp.dot(x, wg_ref[...].astype(BF16), preferred_element_type=F32)
    b = jnp.dot(x, wu_ref[...].astype(BF16), preferred_element_type=F32)
    o_ref[...] = (jax.nn.silu(a) * b).astype(o_ref.dtype)


def _mm_residual_prenorm_kernel(x_ref, w_ref, r_ref, g_ref, o_ref, yg_ref, ssq_ref):
    w = w_ref[...].astype(BF16)
    y = r_ref[...] + jnp.dot(x_ref[...], w, preferred_element_type=F32)
    o_ref[...] = y
    yg_ref[...] = (y * g_ref[...]).astype(yg_ref.dtype)
    sq = y * y
    part = sq[:, :LANES]
    for c in range(1, sq.shape[1] // LANES):
        part = part + sq[:, c * LANES:(c + 1) * LANES]

    @pl.when(pl.program_id(1) == 0)
    def _():
        ssq_ref[...] = part

    @pl.when(pl.program_id(1) > 0)
    def _():
        ssq_ref[...] += part


def _rms_scale_kernel(d_model, ssq_ref, o_ref):
    ms = jnp.sum(ssq_ref[...], axis=-1, keepdims=True) / d_model
    o_ref[...] = jnp.broadcast_to(lax.rsqrt(ms + EPS), o_ref.shape)


def _rms_scale(ssq, d_model, tm=1024):
    m = ssq.shape[0]
    spec = pl.BlockSpec((tm, LANES), lambda i: (i, 0))
    return pl.pallas_call(
        functools.partial(_rms_scale_kernel, d_model),
        grid=(m // tm,),
        in_specs=[spec],
        out_specs=spec,
        out_shape=jax.ShapeDtypeStruct((m, LANES), F32),
        compiler_params=_cparams(("parallel",)),
        name="rms_scale",
    )(ssq)


def _lane_tiled(scale_ref, width):
    r = scale_ref[...]
    return jnp.concatenate([r] * (width // LANES), axis=1)


def _mm_postnorm_kernel(x_ref, w_ref, scale_ref, o_ref):
    r = _lane_tiled(scale_ref, o_ref.shape[1])
    w = w_ref[...].astype(BF16)
    acc = jnp.dot(x_ref[...], w, preferred_element_type=F32)
    o_ref[...] = (r * acc).astype(o_ref.dtype)


def _gate_up_postnorm_kernel(x_ref, wg_ref, wu_ref, scale_ref, o_ref):
    tm, tn = o_ref.shape
    wg = wg_ref[...].astype(BF16)
    wu = wu_ref[...].astype(BF16)
    rc = tm // GATE_UP_ROW_CHUNKS
    for c in range(GATE_UP_ROW_CHUNKS):
        rows = pl.ds(c * rc, rc)
        r = _lane_tiled(scale_ref.at[rows, :], tn)
        x = x_ref[rows, :]
        a = r * jnp.dot(x, wg, preferred_element_type=F32)
        b = r * jnp.dot(x, wu, preferred_element_type=F32)
        o_ref[rows, :] = (jax.nn.silu(a) * b).astype(o_ref.dtype)


def _weight_spec(w, layer, tn, col_block0=0):
    return pl.BlockSpec((None, w.shape[1], tn), lambda i, j: (layer, 0, col_block0 + j))


def _matmul(x, w, layer, tm, tn, n_out=None, col_block0=0, scale=None, out_dtype=F32, name="matmul"):
    m, k = x.shape
    n = w.shape[2] if n_out is None else n_out
    in_specs = [pl.BlockSpec((tm, k), lambda i, j: (i, 0)),
                _weight_spec(w, layer, tn, col_block0)]
    args = [x, w]
    body = _mm_kernel
    if scale is not None:
        in_specs.append(pl.BlockSpec((tm, LANES), lambda i, j: (i, 0)))
        args.append(scale)
        body = _mm_postnorm_kernel
    return pl.pallas_call(
        body,
        grid=(m // tm, n // tn),
        in_specs=in_specs,
        out_specs=pl.BlockSpec((tm, tn), lambda i, j: (i, j)),
        out_shape=jax.ShapeDtypeStruct((m, n), out_dtype),
        compiler_params=_cparams(("parallel", "arbitrary")),
        name=name,
    )(*args)


def _matmul_residual(x, w, layer, res, tm, tn, single_buffer_lhs=False, next_norm_w=None,
                     name="matmul_residual"):
    m, k = x.shape
    n = w.shape[2]
    lhs_kwargs = {"pipeline_mode": pl.Buffered(1)} if single_buffer_lhs else {}
    tile = pl.BlockSpec((tm, tn), lambda i, j: (i, j))
    in_specs = [pl.BlockSpec((tm, k), lambda i, j: (i, 0), **lhs_kwargs),
                _weight_spec(w, layer, tn), tile]
    args = [x, w, res]
    if next_norm_w is None:
        body, out_specs, out_shape = _mm_residual_kernel, tile, jax.ShapeDtypeStruct((m, n), F32)
    else:
        in_specs.append(pl.BlockSpec((1, tn), lambda i, j: (0, j)))
        args.append(next_norm_w.reshape(1, n).astype(F32))
        body = _mm_residual_prenorm_kernel
        out_specs = [tile, tile, pl.BlockSpec((tm, LANES), lambda i, j: (i, 0))]
        out_shape = [jax.ShapeDtypeStruct((m, n), F32), jax.ShapeDtypeStruct((m, n), BF16),
                     jax.ShapeDtypeStruct((m, LANES), F32)]
    return pl.pallas_call(
        body,
        grid=(m // tm, n // tn),
        in_specs=in_specs,
        out_specs=out_specs,
        out_shape=out_shape,
        compiler_params=_cparams(("parallel", "arbitrary")),
        name=name,
    )(*args)


def _gate_up(x, wg, wu, layer, tm, tn, scale):
    m, k = x.shape
    n = wg.shape[2]
    return pl.pallas_call(
        _gate_up_postnorm_kernel,
        grid=(m // tm, n // tn),
        in_specs=[pl.BlockSpec((tm, k), lambda i, j: (i, 0), pipeline_mode=pl.Buffered(1)),
                  _weight_spec(wg, layer, tn),
                  _weight_spec(wu, layer, tn),
                  pl.BlockSpec((tm, LANES), lambda i, j: (i, 0))],
        out_specs=pl.BlockSpec((tm, tn), lambda i, j: (i, j)),
        out_shape=jax.ShapeDtypeStruct((m, n), BF16),
        compiler_params=_cparams(("parallel", "arbitrary")),
        name="gate_up",
    )(x, wg, wu, scale)


def _dot_nt(a, b):
    return lax.dot_general(a, b, (((1,), (1,)), ((), ())), preferred_element_type=F32)


def _dot_tn(a, b):
    return lax.dot_general(a, b, (((0,), (0,)), ((), ())), preferred_element_type=F32)


def _dot(a, b):
    return jnp.dot(a, b, preferred_element_type=F32)


def _head_norm(x):
    return x * lax.rsqrt(jnp.mean(x * x, axis=-1, keepdims=True) + EPS)


def _col_to_row(col, eye):
    return jnp.sum(jnp.where(eye, col, 0.0), axis=0, keepdims=True)


def _row_to_col(row, eye):
    return jnp.sum(jnp.where(eye, row, 0.0), axis=1, keepdims=True)


def _alias_existing(mixed, in_specs, args):
    if isinstance(mixed, jax.ShapeDtypeStruct):
        return {}
    in_specs.append(pl.BlockSpec(memory_space=pl.ANY))
    args.append(mixed)
    return {len(args) - 1: 0}


def _rotary(x, cos, sin):
    half = x.shape[-1] // 2
    x1, x2 = x[:, :half], x[:, half:]
    return jnp.concatenate([x1 * cos - x2 * sin, x1 * sin + x2 * cos], axis=-1)


def _ret_kernel(L, units, carry, unroll, *refs):
    (q_ref, k_ref, v_ref, g_ref, cos_ref, sin_ref, dec_ref, qd_ref, kd_ref, cd_ref) = refs[:10]
    s0_ref = None if carry else refs[10]
    o_ref, sout_ref = refs[-2:]

    if carry:
        @pl.when(pl.program_id(2) == 0)
        def _():
            sout_ref[...] = jnp.zeros_like(sout_ref)

    dec = dec_ref[...]
    qd = qd_ref[...]
    kd = kd_ref[...]
    cd = cd_ref[...]
    scale = HEAD_DIM_AB ** -0.5

    def unit(u, c):
        rows = pl.ds(pl.multiple_of(u * L, L), L)
        if carry:
            cos, sin = cos_ref[rows, :], sin_ref[rows, :]
            s = sout_ref[...]
        else:
            cos, sin = cos_ref[...], sin_ref[...]
            s = s0_ref[u]
        q = _rotary(q_ref[rows, :], cos, sin)
        k = _rotary(k_ref[rows, :], cos, sin) * scale
        v = v_ref[rows, :]
        att = _dot_nt(q, k) * dec
        o = _dot(att, v) + _dot(q, s) * qd
        s_new = cd * s + _dot_tn(k * kd, v)
        out = jax.nn.silu(g_ref[rows, :]) * _head_norm(o)
        o_ref[rows, :] = out.astype(o_ref.dtype)
        if carry:
            sout_ref[...] = s_new
        else:
            sout_ref[u] = s_new
        return c

    lax.fori_loop(0, units, unit, 0, unroll=unroll)


def _retention_tables(L):
    log_gamma = jnp.log1p(-jnp.exp2(-5.0 - jnp.arange(RET_HEADS, dtype=F32)))[:, None]
    idx = jnp.arange(L, dtype=F32)
    diff = idx[:, None] - idx[None, :]
    decay = jnp.exp(jnp.where(diff >= 0, log_gamma[:, :, None] * diff, -jnp.inf))
    q_decay = jnp.exp(log_gamma * (idx + 1.0))[:, :, None]
    k_decay = jnp.exp(log_gamma * (L - 1.0 - idx))[:, :, None]
    chunk_decay = jnp.exp(log_gamma[:, 0] * L)[:, None, None]
    return decay, q_decay, k_decay, chunk_decay


def _rope_tables(pos):
    half = HEAD_DIM_AB // 2
    inv_freq = ROPE_BASE ** (-jnp.arange(half, dtype=F32) / half)
    ang = pos[:, None] * inv_freq[None, :]
    return jnp.cos(ang), jnp.sin(ang)


def _retention(proj, mixed, *, row0, batch, seq, L, rows_per_step, pos0, s0=None, unroll=1):
    d = HEAD_DIM_AB
    h_ = RET_HEADS
    carry = s0 is None
    units = rows_per_step // L
    decay, q_decay, k_decay, chunk_decay = _retention_tables(L)
    cos, sin = _rope_tables(pos0 + jnp.arange(seq, dtype=F32))
    rb0 = row0 // rows_per_step
    if carry:
        tblocks = seq // rows_per_step
        grid = (batch, h_, tblocks)
        row_idx = lambda b, h, t: rb0 + b * tblocks + t
        rope_spec = pl.BlockSpec((rows_per_step, d // 2), lambda b, h, t: (t, 0))
        sout_spec = pl.BlockSpec((None, None, d, d), lambda b, h, t: (b, h, 0, 0))
    else:
        nb = rows_per_step // seq
        grid = (batch // nb, h_, 1)
        row_idx = lambda b, h, t: rb0 + b
        rope_spec = pl.BlockSpec((seq, d // 2), lambda b, h, t: (0, 0))
        sout_spec = pl.BlockSpec((nb, None, d, d), lambda b, h, t: (b, h, 0, 0))

    def col_spec(c0):
        return pl.BlockSpec((rows_per_step, d), lambda b, h, t: (row_idx(b, h, t), c0 + h))

    in_specs = [col_spec(0), col_spec(h_), col_spec(2 * h_), col_spec(3 * h_),
                rope_spec, rope_spec,
                pl.BlockSpec((None, L, L), lambda b, h, t: (h, 0, 0)),
                pl.BlockSpec((None, L, 1), lambda b, h, t: (h, 0, 0)),
                pl.BlockSpec((None, L, 1), lambda b, h, t: (h, 0, 0)),
                pl.BlockSpec((None, 1, 1), lambda b, h, t: (h, 0, 0))]
    args = [proj, proj, proj, proj, cos, sin, decay, q_decay, k_decay, chunk_decay]
    if not carry:
        in_specs.append(sout_spec)
        args.append(s0)
    aliases = _alias_existing(mixed, in_specs, args)
    return pl.pallas_call(
        functools.partial(_ret_kernel, L, units, carry, unroll),
        grid=grid,
        in_specs=in_specs,
        out_specs=[col_spec(0), sout_spec],
        out_shape=[jax.ShapeDtypeStruct(mixed.shape, mixed.dtype),
                   jax.ShapeDtypeStruct((batch, h_, d, d), F32)],
        input_output_aliases=aliases,
        compiler_params=_recurrence_cparams(),
        name="retention_prompt" if carry else "retention_sample",
    )(*args)


def _mlstm_kernel(L, units, carry, unroll, *refs):
    (q_ref, k_ref, v_ref, og_ref, gates_ref, bias_ref, nw_ref) = refs[:7]
    c0_ref, n0_ref, m0_ref = (None, None, None) if carry else refs[7:10]
    o_ref, c_ref, n_ref, m_ref = refs[-4:]

    if carry:
        @pl.when(pl.program_id(2) == 0)
        def _():
            c_ref[...] = jnp.zeros_like(c_ref)
            n_ref[...] = jnp.zeros_like(n_ref)
            m_ref[...] = jnp.zeros_like(m_ref)

    head = pl.program_id(1)
    bias = bias_ref[...]
    nw = nw_ref[...]
    lane = lax.broadcasted_iota(jnp.int32, (L, GATE_LANES), 1)
    ti = lax.broadcasted_iota(jnp.int32, (L, L), 0)
    si = lax.broadcasted_iota(jnp.int32, (L, L), 1)
    eye = ti == si
    causal = si <= ti
    scale = HEAD_DIM_AB ** -0.5

    def gate_terms(rows, m):
        x = gates_ref[rows, :] + bias
        ig = jnp.sum(jnp.where(lane == head, x, 0.0), axis=1, keepdims=True)
        fpre = jnp.sum(jnp.where(lane == head + ML_HEADS, x, 0.0), axis=1, keepdims=True)
        lf = jnp.minimum(fpre, 0.0) - jnp.log1p(jnp.exp(-jnp.abs(fpre)))
        f_cum = jnp.sum(jnp.where(causal, _col_to_row(lf, eye), 0.0), axis=1, keepdims=True)
        a = ig - f_cum
        a_row = _col_to_row(a, eye)
        cmax = jnp.max(jnp.where(causal, a_row, -jnp.inf), axis=1, keepdims=True)
        m_t = f_cum + jnp.maximum(m, cmax)
        f_end = f_cum[L - 1:L, :]
        m_end = m_t[L - 1:L, :]
        return dict(
            d_mat=jnp.exp(jnp.where(causal, (f_cum - m_t) + a_row, -jnp.inf)),
            inter=jnp.exp(f_cum + m - m_t),
            floor=jnp.exp(-m_t),
            w=jnp.exp(f_end - m_end + a),
            cdec=jnp.exp(f_end + m - m_end),
            m_end=m_end)

    def gate_free_products(rows, c):
        q = q_ref[rows, :]
        return _dot_nt(q, k_ref[rows, :] * scale), (None if c is None else _dot(q, c))

    def mix(rows, g, c, n, products):
        q = q_ref[rows, :]
        k = k_ref[rows, :] * scale
        v = v_ref[rows, :]
        qk, qc = products
        if qc is None:
            qc = _dot(q, c)
        s = qk * g["d_mat"]
        num = g["inter"] * qc + _dot(s, v)
        den = (g["inter"] * jnp.sum(q * n, axis=1, keepdims=True)
               + jnp.sum(s, axis=1, keepdims=True))
        hh = num * (1.0 / jnp.maximum(jnp.abs(den), g["floor"]))
        kw = k * g["w"]
        c_new = g["cdec"] * c + _dot_tn(kw, v)
        n_new = g["cdec"] * n + jnp.sum(kw, axis=0, keepdims=True)
        out = jax.nn.sigmoid(og_ref[rows, :]) * (_head_norm(hh) * nw)
        o_ref[rows, :] = out.astype(o_ref.dtype)
        return c_new, n_new

    if carry:
        def group(i, m):
            chunks = []
            for j in range(unroll):
                rows = pl.ds(pl.multiple_of((i * unroll + j) * L, L), L)
                g = gate_terms(rows, m)
                m = g["m_end"]
                chunks.append((rows, g))
            for rows, g in chunks:
                c_ref[...], n_ref[...] = mix(rows, g, c_ref[...], n_ref[...],
                                             gate_free_products(rows, None))
            return m

        m_last = lax.fori_loop(0, units // unroll, group, m_ref[:, 0:1])
        m_ref[...] = jnp.broadcast_to(m_last, m_ref.shape)
    else:
        early =[gate_free_products(pl.ds(u * L, L), c0_ref[u]) for u in range(units)]
        terms = [gate_terms(pl.ds(u * L, L), m0_ref[u][:, 0:1]) for u in range(units)]
        for u in range(units):
            rows = pl.ds(u * L, L)
            g = terms[u]
            c_ref[u], n_ref[u] = mix(rows, g, c0_ref[u], n0_ref[u], early[u])
            m_ref[u] = jnp.broadcast_to(g["m_end"], (1, GATE_LANES))


def _mlstm(proj, gates, bias, norm_w, mixed, *, row0, batch, seq, L, rows_per_step,
           c0=None, n0=None, m0=None, unroll=1):
    d = HEAD_DIM_AB
    h_ = ML_HEADS
    carry = c0 is None
    units = rows_per_step // L
    rb0 = row0 // rows_per_step
    col0 = 4 * RET_HEADS
    if carry:
        tblocks = seq // rows_per_step
        grid = (batch, h_, tblocks)
        row_idx = lambda b, h, t: rb0 + b * tblocks + t
        lead = None
    else:
        nb = rows_per_step // seq
        grid = (batch // nb, h_, 1)
        row_idx = lambda b, h, t: rb0 + b
        lead = nb
    c_spec = pl.BlockSpec((lead, None, d, d), lambda b, h, t: (b, h, 0, 0))
    n_spec = pl.BlockSpec((lead, None, 1, d), lambda b, h, t: (b, h, 0, 0))
    m_spec = pl.BlockSpec((lead, None, 1, GATE_LANES), lambda b, h, t: (b, h, 0, 0))

    def col_spec(c_blk):
        return pl.BlockSpec((rows_per_step, d), lambda b, h, t: (row_idx(b, h, t), c_blk + h))

    in_specs = [col_spec(col0), col_spec(col0 + h_), col_spec(col0 + 2 * h_), col_spec(col0 + 3 * h_),
                pl.BlockSpec((rows_per_step, GATE_LANES), lambda b, h, t: (row_idx(b, h, t), 0)),
                pl.BlockSpec((1, GATE_LANES), lambda b, h, t: (0, 0)),
                pl.BlockSpec((1, d), lambda b, h, t: (0, h))]
    args = [proj, proj, proj, proj, gates, bias, norm_w]
    if not carry:
        in_specs += [c_spec, n_spec, m_spec]
        args += [c0, n0, m0]
    aliases = _alias_existing(mixed, in_specs, args)
    return pl.pallas_call(
        functools.partial(_mlstm_kernel, L, units, carry, unroll),
        grid=grid,
        in_specs=in_specs,
        out_specs=[col_spec(RET_HEADS), c_spec, n_spec, m_spec],
        out_shape=[jax.ShapeDtypeStruct(mixed.shape, mixed.dtype),
                   jax.ShapeDtypeStruct((batch, h_, d, d), F32),
                   jax.ShapeDtypeStruct((batch, h_, 1, d), F32),
                   jax.ShapeDtypeStruct((batch, h_, 1, GATE_LANES), F32)],
        input_output_aliases=aliases,
        compiler_params=_recurrence_cparams(),
        name="mlstm_prompt" if carry else "mlstm_sample",
    )(*args)


def _hgrn_kernel(L, units, carry, unroll, *refs):
    (q_ref, f_ref, i_ref, g_ref, lbl_ref, nw_ref) = refs[:6]
    rows_ref = refs[-1]
    if carry:
        s0_ref = None
        o_ref, sout_ref, st_ref = refs[-4:-1]
    else:
        s0_ref = refs[6]
        o_ref, sout_ref, kd_ref, eg_ref = refs[-5:-1]
        st_ref = None

    if carry:
        @pl.when(pl.program_id(2) == 0)
        def _():
            st_ref[...] = jnp.zeros_like(st_ref)

    d = HG_DIM
    lg = lbl_ref[...]
    e = jnp.exp(lg - jnp.max(lg, axis=0, keepdims=True))
    sm = e / jnp.sum(e, axis=0, keepdims=True)
    lb = (sm[0:1, :] + sm[1:2, :]) - sm[0:1, :]
    nw = nw_ref[...]
    groups = L // 8
    pairs = [(j, jp) for jp in range(groups - 1) for j in range(jp + 1, groups)]
    row8 = lax.broadcasted_iota(jnp.int32, (8, d), 0)
    lane = lax.broadcasted_iota(jnp.int32, (8, L), 1)
    if not carry:
        eye_d = (lax.broadcasted_iota(jnp.int32, (d, d), 0)
                 == lax.broadcasted_iota(jnp.int32, (d, d), 1))

    def unit(u, carry_val):
        rows = pl.ds(pl.multiple_of(u * L, L), L)
        q = jax.nn.silu(q_ref[rows, :])
        fg = lb + (1.0 - lb) * jax.nn.sigmoid(f_ref[rows, :])
        k = 1.0 - fg
        logf = jnp.log(fg)
        v = i_ref[rows, :]
        gg = []
        base = jnp.zeros((1, d), F32)
        for j in range(groups):
            cum = logf[8 * j:8 * j + 8, :]
            for sh in (1, 2, 4):
                cum = cum + jnp.where(row8 >= sh, pltpu.roll(cum, sh, 0), 0.0)
            cum = cum + base
            gg.append(cum)
            base = cum[7:8, :]
        g_end = base
        g_cum = jnp.concatenate(gg, axis=0)
        qg = [q[8 * j:8 * j + 8, :] for j in range(groups)]
        kg = [k[8 * j:8 * j + 8, :] for j in range(groups)]
        rows_ref[0, rows, :] = g_cum
        rows_ref[1, rows, :] = k
        row0 = pl.multiple_of(u * L, L)
        att = []
        for j in range(groups):
            a = jnp.zeros((8, L), F32)
            for r in range(8):
                g_r = rows_ref[0, pl.ds(row0 + 8 * j + r, 1), :]
                k_r = rows_ref[1, pl.ds(row0 + 8 * j + r, 1), :]
                diff = jnp.where(row8 >= r, gg[j] - g_r, -jnp.inf)
                col = jnp.sum(qg[j] * k_r * jnp.exp(diff), axis=1, keepdims=True)
                a = jnp.where(lane == 8 * j + r, col, a)
            att.append(a)
        if pairs:
            ends = [gg[j][7:8, :] for j in range(groups)]
            kt = jnp.concatenate([kg[j] * jnp.exp(ends[j] - gg[j]) for j in range(groups)], axis=0)
            qt = jnp.concatenate([qg[j] * jnp.exp(gg[j] - ends[jp]) for j, jp in pairs], axis=0)
            cross = _dot_nt(qt, kt)
            for idx, (j, jp) in enumerate(pairs):
                in_group = (lane >= 8 * jp) & (lane < 8 * jp + 8)
                att[j] = jnp.where(in_group, cross[8 * idx:8 * idx + 8, :], att[j])
        att_m = jnp.concatenate(att, axis=0)
        qs = q * jnp.exp(g_cum)
        kd = k * jnp.exp(g_end - g_cum)
        if carry:
            st = st_ref[...]
            o = _dot_nt(qs, st) + _dot(att_m, v)
            st_ref[...] = st * jnp.exp(g_end) + _dot_tn(v, kd)
        else:
            o = _dot(qs, s0_ref[u]) + _dot(att_m, v)
            kd_ref[rows, :] = kd
            eg_ref[pl.ds(u, 1), :] = jnp.exp(g_end)
        out = (_head_norm(o) * nw) * jax.nn.silu(g_ref[rows, :])
        o_ref[rows, :] = out.astype(o_ref.dtype)
        return carry_val

    lax.fori_loop(0, units, unit, 0, unroll=unroll)

    if carry:
        @pl.when(pl.program_id(2) == pl.num_programs(2) - 1)
        def _():
            sout_ref[...] = st_ref[...].T
    else:
        per_tile = d // L
        row_chunk = lax.broadcasted_iota(jnp.int32, (d, d), 0) // L
        for t in range(units // per_tile):
            kd_t = kd_ref[t * d:(t + 1) * d, :].T
            v_t = i_ref[t * d:(t + 1) * d, :]
            for c in range(per_tile):
                u = t * per_tile + c
                upd = _dot(kd_t, jnp.where(row_chunk == c, v_t, 0.0))
                sout_ref[u] = _row_to_col(eg_ref[u:u + 1, :], eye_d) * s0_ref[u] + upd


def _hgrn(proj, lb_logits, norm_w, mixed, *, row0, batch, seq, L, rows_per_step, s0=None, unroll=1):
    d = HG_DIM
    h_ = proj.shape[1] // (4 * d)
    carry = s0 is None
    units = rows_per_step // L
    rb0 = row0 // rows_per_step
    if carry:
        tblocks = seq // rows_per_step
        grid = (batch, h_, tblocks)
        row_idx = lambda b, h, t: rb0 + b * tblocks + t
        sout_spec = pl.BlockSpec((None, None, d, d), lambda b, h, t: (b, h, 0, 0))
        scratch = [pltpu.VMEM((d, d), F32), pltpu.VMEM((2, rows_per_step, d), F32)]
    else:
        nb = rows_per_step // seq
        grid = (batch // nb, h_, 1)
        row_idx = lambda b, h, t: rb0 + b
        sout_spec = pl.BlockSpec((nb, None, d, d), lambda b, h, t: (b, h, 0, 0))
        scratch = [pltpu.VMEM((rows_per_step, d), F32), pltpu.VMEM((units, d), F32),
                   pltpu.VMEM((2, rows_per_step, d), F32)]

    def col_spec(c0):
        return pl.BlockSpec((rows_per_step, d), lambda b, h, t: (row_idx(b, h, t), c0 + h))

    in_specs = [col_spec(0), col_spec(h_), col_spec(2 * h_), col_spec(3 * h_),
                pl.BlockSpec((2, d), lambda b, h, t: (0, h)),
                pl.BlockSpec((1, d), lambda b, h, t: (0, 0))]
    args = [proj, proj, proj, proj, lb_logits, norm_w]
    if not carry:
        in_specs.append(sout_spec)
        args.append(s0)
    aliases = _alias_existing(mixed, in_specs, args)
    return pl.pallas_call(
        functools.partial(_hgrn_kernel, L, units, carry, unroll),
        grid=grid,
        in_specs=in_specs,
        out_specs=[col_spec(0), sout_spec],
        out_shape=[jax.ShapeDtypeStruct(mixed.shape, mixed.dtype),
                   jax.ShapeDtypeStruct((batch, h_, d, d), F32)],
        scratch_shapes=scratch,
        input_output_aliases=aliases,
        compiler_params=_recurrence_cparams(),
        name="hgrn_prompt" if carry else "hgrn_sample",
    )(*args)


def _chunk_len(t, c):
    return c if t % c == 0 else t


def _swiglu_residual(x, xg, ssq, w_gate, w_up, w_down, layer, next_norm_w):
    act = _gate_up(xg, w_gate, w_up, layer, TM_GATE_UP, 256, _rms_scale(ssq, x.shape[1]))
    return _matmul_residual(act, w_down, layer, x, TM_DOWN, 256, single_buffer_lhs=True,
                            next_norm_w=next_norm_w, name="down_proj")


def kernel(x_prompt, x_sample, state_ret, state_mlstm_C, state_mlstm_n, state_mlstm_m, state_hgrn,
           norm_mix_w, w_in_ab, b_if_ab, ml_norm_w, w_out_ab, w_in_c, lb_logits, hg_norm_w, w_out_c,
           norm_ffn_w, w_gate, w_up, w_down, norm_final_w):
    bp, tp, d = x_prompt.shape
    bs, ts, _ = x_sample.shape
    mp, ms = bp * tp, bs * ts
    tm = TM_WIDE

    w_in_ab_b, w_out_ab_b = w_in_ab.astype(BF16), w_out_ab
    w_in_c_b, w_out_c_b = w_in_c, w_out_c
    w_gate_b, w_up_b, w_down_b = w_gate, w_up, w_down

    ab_main = 4 * RET_HEADS * HEAD_DIM_AB + 4 * ML_HEADS * HEAD_DIM_AB
    x, hn = _rmsnorm_stack([x_prompt.reshape(mp, d), x_sample.reshape(ms, d)], norm_mix_w[0])
    proj = _matmul(hn, w_in_ab_b, 0, tm, 512, n_out=ab_main, name="in_proj_ab")
    gates = _matmul(hn, w_in_ab_b, 0, tm, GATE_LANES, n_out=GATE_LANES,
                    col_block0=ab_main // GATE_LANES, name="in_proj_gates")
    bias = jnp.pad(b_if_ab[0].astype(F32), (0, GATE_LANES - 2 * ML_HEADS)).reshape(1, GATE_LANES)
    ml_w = ml_norm_w[0].astype(F32).reshape(1, -1)

    mixed = jax.ShapeDtypeStruct((mp + ms, d), BF16)
    lp = _chunk_len(tp, RET_CHUNK)
    ls = _chunk_len(ts, RET_CHUNK)
    mixed, ret_p = _retention(proj, mixed, row0=0, batch=bp, seq=tp, L=lp, rows_per_step=8 * lp,
                              pos0=0.0, unroll=UNROLL_AB_PROMPT)
    mixed, ret_s = _retention(proj, mixed, row0=mp, batch=bs, seq=ts, L=ls, rows_per_step=16 * ls,
                              pos0=float(PAST_LEN), s0=state_ret[0], unroll=UNROLL_AB_SAMPLE)
    lp = _chunk_len(tp, ML_CHUNK)
    ls = _chunk_len(ts, ML_CHUNK)
    mixed, mc_p, mn_p, mm_p = _mlstm(proj, gates, bias, ml_w, mixed, row0=0, batch=bp, seq=tp, L=lp,
                                     rows_per_step=8 * lp, unroll=4)
    m0 =jnp.broadcast_to(state_mlstm_m[0][:, :, None, None], (bs, ML_HEADS, 1, GATE_LANES))
    mixed, mc_s, mn_s, mm_s = _mlstm(proj, gates, bias, ml_w, mixed, row0=mp, batch=bs, seq=ts, L=ls,
                                     rows_per_step=8 * ls, c0=state_mlstm_C[0],
                                     n0=state_mlstm_n[0][:, :, None, :], m0=m0, unroll=UNROLL_AB_SAMPLE)
    x, xg, ssq = _matmul_residual(mixed, w_out_ab_b, 0, x, TM_OUT, 512, next_norm_w=norm_ffn_w[0],
                                  name="out_proj_ab")
    x, xg, ssq = _swiglu_residual(x, xg, ssq, w_gate_b, w_up_b, w_down_b, 0, norm_mix_w[1])

    proj = _matmul(xg, w_in_c_b, 0, tm, 512, scale=_rms_scale(ssq, d), name="in_proj_c")
    mixed = jax.ShapeDtypeStruct((mp + ms, d), BF16)
    hg_w =hg_norm_w[0].astype(F32).reshape(1, -1)
    lbl = lb_logits.astype(F32)
    lp = _chunk_len(tp, HG_CHUNK)
    ls = _chunk_len(ts, HG_CHUNK)
    mixed, hg_p = _hgrn(proj, lbl, hg_w, mixed, row0=0, batch=bp, seq=tp, L=lp, rows_per_step=32 * lp,
                        unroll=UNROLL_HG_PROMPT)
    mixed, hg_s = _hgrn(proj, lbl, hg_w, mixed, row0=mp, batch=bs, seq=ts, L=ls, rows_per_step=32 * ls,
                        s0=state_hgrn[0], unroll=UNROLL_HG_SAMPLE)
    x, xg, ssq = _matmul_residual(mixed, w_out_c_b, 0, x, TM_OUT, 512, next_norm_w=norm_ffn_w[1],
                                  name="out_proj_c")
    x = _swiglu_residual(x, xg, ssq, w_gate_b, w_up_b, w_down_b, 1, None)

    y_prompt = _rmsnorm(x, norm_final_w, F32, row0=0, rows=mp).reshape(bp, tp, d)
    y_sample = _rmsnorm(x, norm_final_w, F32, row0=mp, rows=ms).reshape(bs, ts, d)
    return (y_prompt, y_sample,
            ret_p[None], mc_p[None], mn_p[:, :, 0, :][None], mm_p[:, :, 0, 0][None], hg_p[None],
            ret_s[None], mc_s[None], mn_s[:, :, 0, :][None], mm_s[:, :, 0, 0][None], hg_s[None])
```
